```python
import jax, jax.numpy as jnp
from jax import lax
import numpy as np

D_MODEL = 2048
BATCH = 1
SEQ = 8192
DEPTH = 4

CHUNK = 64
N_A_LAYERS = DEPTH // 2
N_B_LAYERS = DEPTH - N_A_LAYERS
POOL_WINDOWS = (2, 4, 8, 16)
N_POOL_GROUPS = len(POOL_WINDOWS)
POOL_GROUP = D_MODEL // N_POOL_GROUPS
N_HEADS = 16
HEAD_DIM = D_MODEL // N_HEADS
LEFT_CHUNKS = 8
LEFT = LEFT_CHUNKS * CHUNK
BAND = (LEFT_CHUNKS + 1) * CHUNK
REL_MAX = 128
N_REL = (CHUNK - 1) + REL_MAX + 1
D_FF = 5504
EPS = 1e-6
NEG_INF = -1e30

kernel_name = "yoco_pool_chunked_relbias_macaron"


def rmsnorm(x, g):
    xf = x.astype(jnp.float32)
    y = xf * lax.rsqrt(jnp.mean(xf * xf, axis=-1, keepdims=True) + EPS)
    return (y * g.astype(jnp.float32)).astype(x.dtype)


def swiglu(h, w_gate, w_up, w_down):
    return (jax.nn.silu(h @ w_gate) * (h @ w_up)) @ w_down


def pool_mixer(h, w_pool, scale):
    B, S, D = h.shape
    hf = h.astype(jnp.float32).reshape(B, S, N_POOL_GROUPS, POOL_GROUP)
    cs = jnp.cumsum(hf, axis=1)
    t = jnp.arange(S)
    pooled = []
    for g, w in enumerate(POOL_WINDOWS):
        csg = cs[:, :, g]
        prev = jnp.pad(csg, ((0, 0), (w, 0), (0, 0)))[:, :S]
        cnt = jnp.minimum(t + 1, w).astype(jnp.float32)[None, :, None]
        pooled.append((csg - prev) / cnt)
    pooled = jnp.stack(pooled, axis=2)
    diff = (pooled - hf).astype(h.dtype)
    y = jnp.einsum('bsgc,gcd->bsgd', diff, w_pool).reshape(B, S, D)
    return y * scale


def head_rmsnorm(t, g):
    tf = t.astype(jnp.float32)
    y = tf * lax.rsqrt(jnp.mean(tf * tf, axis=-1, keepdims=True) + EPS)
    return (y * g.astype(jnp.float32)).astype(t.dtype)


def shared_kv(x, kv_norm, w_k, w_v, k_gain):
    B, S, _ = x.shape
    hk = rmsnorm(x, kv_norm)
    k = head_rmsnorm((hk @ w_k).reshape(B, S, N_HEADS, HEAD_DIM), k_gain)
    v = (hk @ w_v).reshape(B, S, N_HEADS, HEAD_DIM)
    pad = ((0, 0), (LEFT, 0), (0, 0), (0, 0))
    return jnp.pad(k, pad), jnp.pad(v, pad)


def chunked_attention(h, w_q, q_gain, rel_table, w_o, k_pad, v_pad):
    B, S, D = h.shape
    nc = S // CHUNK
    q = head_rmsnorm((h @ w_q).reshape(B, S, N_HEADS, HEAD_DIM), q_gain)
    qc = q.reshape(B, nc, CHUNK, N_HEADS, HEAD_DIM).transpose(1, 0, 2, 3, 4)
    r = jnp.arange(CHUNK)[:, None]
    m = jnp.arange(BAND)[None, :]
    rel = r - m + LEFT
    idx = jnp.clip(rel, -(CHUNK - 1), REL_MAX) + (CHUNK - 1)
    bias = rel_table.astype(jnp.float32)[:, idx]
    scale = HEAD_DIM ** -0.5
    key_off = jnp.arange(BAND)

    def one_chunk(args):
        q_blk, c = args
        start = c * CHUNK
        kb = lax.dynamic_slice_in_dim(k_pad, start, BAND, axis=1)
        vb = lax.dynamic_slice_in_dim(v_pad, start, BAND, axis=1)
        s = jnp.einsum('bqhd,bkhd->bhqk', q_blk.astype(jnp.float32), kb.astype(jnp.float32)) * scale
        s = s + bias[None]
        valid = (start + key_off) >= LEFT
        s = jnp.where(valid[None, None, None, :], s, NEG_INF)
        p = jax.nn.softmax(s, axis=-1)
        o = jnp.einsum('bhqk,bkhd->bqhd', p, vb.astype(jnp.float32))
        return o.astype(h.dtype)

    out = lax.map(one_chunk, (qc, jnp.arange(nc)))
    out = out.transpose(1, 0, 2, 3, 4).reshape(B, S, D)
    return out @ w_o


def setup_inputs(seed: int = 0) -> dict:
    key = jax.random.key(seed)
    ks = jax.random.split(key, 24)
    f32 = jnp.float32

    def nrm(k, shape, s):
        return jax.random.normal(k, shape, f32) * s

    def gain(k, shape):
        return jnp.ones(shape, f32) + 0.05 * jax.random.normal(k, shape, f32)

    return {
        "x": jax.random.normal(ks[0], (BATCH, SEQ, D_MODEL), f32),
        "ffn1_norm": gain(ks[1], (DEPTH, D_MODEL)),
        "ffn1_w_gate": nrm(ks[2], (DEPTH, D_MODEL, D_FF), D_MODEL ** -0.5),
        "ffn1_w_up": nrm(ks[3], (DEPTH, D_MODEL, D_FF), D_MODEL ** -0.5),
        "ffn1_w_down": nrm(ks[4], (DEPTH, D_FF, D_MODEL), D_FF ** -0.5),
        "mix_norm": gain(ks[5], (DEPTH, D_MODEL)),
        "ffn2_norm": gain(ks[6], (DEPTH, D_MODEL)),
        "ffn2_w_gate": nrm(ks[7], (DEPTH, D_MODEL, D_FF), D_MODEL ** -0.5),
        "ffn2_w_up": nrm(ks[8], (DEPTH, D_MODEL, D_FF), D_MODEL ** -0.5),
        "ffn2_w_down": nrm(ks[9], (DEPTH, D_FF, D_MODEL), D_FF ** -0.5),
        "pool_w": nrm(ks[10], (N_A_LAYERS, N_POOL_GROUPS, POOL_GROUP, POOL_GROUP), POOL_GROUP ** -0.5),
        "pool_scale": gain(ks[11], (N_A_LAYERS, D_MODEL)),
        "kv_norm": gain(ks[12], (D_MODEL,)),
        "w_k": nrm(ks[13], (D_MODEL, D_MODEL), D_MODEL ** -0.5),
        "w_v": nrm(ks[14], (D_MODEL, D_MODEL), D_MODEL ** -0.5),
        "k_gain": gain(ks[15], (HEAD_DIM,)),
        "w_q": nrm(ks[16], (N_B_LAYERS, D_MODEL, D_MODEL), D_MODEL ** -0.5),
        "q_gain": gain(ks[17], (N_B_LAYERS, HEAD_DIM)),
        "rel_bias": nrm(ks[18], (N_B_LAYERS, N_HEADS, N_REL), 0.5),
        "w_o": nrm(ks[19], (N_B_LAYERS, D_MODEL, D_MODEL), D_MODEL ** -0.5),
    }


def reference(x, ffn1_norm, ffn1_w_gate, ffn1_w_up, ffn1_w_down, mix_norm, ffn2_norm,
              ffn2_w_gate, ffn2_w_up, ffn2_w_down, pool_w, pool_scale, kv_norm, w_k, w_v,
              k_gain, w_q, q_gain, rel_bias, w_o):
    k_pad, v_pad = None, None
    for l in range(DEPTH):
        x = x + 0.5 * swiglu(rmsnorm(x, ffn1_norm[l]), ffn1_w_gate[l], ffn1_w_up[l], ffn1_w_down[l])
        h = rmsnorm(x, mix_norm[l])
        if l < N_A_LAYERS:
            x = x + pool_mixer(h, pool_w[l], pool_scale[l])
        else:
            b = l - N_A_LAYERS
            x = x + chunked_attention(h, w_q[b], q_gain[b], rel_bias[b], w_o[b], k_pad, v_pad)
        x = x + 0.5 * swiglu(rmsnorm(x, ffn2_norm[l]), ffn2_w_gate[l], ffn2_w_up[l], ffn2_w_down[l])
        if l == N_A_LAYERS - 1:
            k_pad, v_pad = shared_kv(x, kv_norm, w_k, w_v, k_gain)
    return x
```

```python
import functools

import jax
import jax.numpy as jnp
from jax import lax
from jax.experimental import pallas as pl
from jax.experimental.pallas import tpu as pltpu

CHUNK = 64
POOL_WINDOWS = (2, 4, 8, 16)
HEAD_DIM = 128
LEFT_CHUNKS = 8
LEFT = LEFT_CHUNKS * CHUNK
REL_MAX = 128
EPS = 1e-6
NEG_INF = -1e30

MXU_DTYPE = jnp.bfloat16
POOL_HALO = 16

FFN_TM = 512
FFN_TF = 1024
PROJ_TM = 512
PROJ_TN = 512
ATTN_TQ = 256
POOL_TM = 256
VMEM_LIMIT = 56 * 1024 * 1024


def _rms(x, g):
    ms = jnp.mean(x * x, axis=-1, keepdims=True)
    return (x * lax.rsqrt(ms + EPS)) * g


def _params(*sem):
    return pltpu.CompilerParams(dimension_semantics=sem, vmem_limit_bytes=VMEM_LIMIT)


def _ffn_kernel(x_ref, g_ref, wg_ref, wu_ref, wd_ref, o_ref, h_ref, *, n_f, tf, rem):
    j = pl.program_id(1)

    @pl.when(j == 0)
    def _():
        h_ref[...] = _rms(x_ref[...], g_ref[...]).astype(h_ref.dtype)
        o_ref[...] = jnp.zeros_like(o_ref)

    def step(width):
        h = h_ref[...]
        gate = jnp.dot(h, wg_ref[:, :width], preferred_element_type=jnp.float32)
        up = jnp.dot(h, wu_ref[:, :width], preferred_element_type=jnp.float32)
        a = (jax.nn.silu(gate) * up).astype(h.dtype)
        o_ref[...] += jnp.dot(a, wd_ref[:width, :], preferred_element_type=jnp.float32)

    @pl.when(j < n_f - 1)
    def _():
        step(tf)

    @pl.when(j == n_f - 1)
    def _():
        step(rem)
        o_ref[...] = x_ref[...] + 0.5 * o_ref[...]


def _ffn(x, gain, wg, wu, wd, layer):
    s, d = x.shape
    f = wg.shape[-1]
    tm, tf = FFN_TM, FFN_TF
    n_f = pl.cdiv(f, tf)
    rem = f - (n_f - 1) * tf
    kern = functools.partial(_ffn_kernel, n_f=n_f, tf=tf, rem=rem)
    return pl.pallas_call(
        kern,
        grid=(s // tm, n_f),
        in_specs=[
            pl.BlockSpec((tm, d), lambda i, j: (i, 0)),
            pl.BlockSpec((None, 1, d), lambda i, j: (layer, 0, 0)),
            pl.BlockSpec((None, d, tf), lambda i, j: (layer, 0, j)),
            pl.BlockSpec((None, d, tf), lambda i, j: (layer, 0, j)),
            pl.BlockSpec((None, tf, d), lambda i, j: (layer, j, 0)),
        ],
        out_specs=pl.BlockSpec((tm, d), lambda i, j: (i, 0)),
        out_shape=jax.ShapeDtypeStruct((s, d), jnp.float32),
        scratch_shapes=[pltpu.VMEM((tm, d), MXU_DTYPE)],
        compiler_params=_params("parallel", "arbitrary"),
        name="ffn",
    )(x, gain, wg, wu, wd)


def _pool_kernel(x_ref, halo_ref, g_ref, w_ref, sc_ref, o_ref, *, tm, group):
    i = pl.program_id(0)
    x = x_ref[...]
    g = g_ref[...]
    h = _rms(x, g)
    hh = _rms(halo_ref[...], g)
    hh = jnp.where(i == 0, 0.0, hh)
    ext = jnp.concatenate([hh, h], axis=0)
    t = i * tm + lax.broadcasted_iota(jnp.int32, (tm, 1), 0)
    for gi, w in enumerate(POOL_WINDOWS):
        cols = slice(gi * group, (gi + 1) * group)
        acc = ext[:, cols]
        k = 1
        while k < w:
            acc = acc + pltpu.roll(acc, k, axis=0)
            k *= 2
        cnt = jnp.minimum(t + 1, w).astype(jnp.float32)
        pooled = acc[POOL_HALO:, :] / cnt
        diff = (pooled - h[:, cols]).astype(w_ref.dtype)
        y = jnp.dot(diff, w_ref[gi], preferred_element_type=jnp.float32)
        o_ref[:, cols] = x[:, cols] + y * sc_ref[:, cols]


def _pool(x, gain, w_pool, scale, layer):
    s, d = x.shape
    n_g, group = w_pool.shape[1], w_pool.shape[2]
    tm = POOL_TM
    hb = tm // POOL_HALO
    kern = functools.partial(_pool_kernel, tm=tm, group=group)
    return pl.pallas_call(
        kern,
        grid=(s // tm,),
        in_specs=[
            pl.BlockSpec((tm, d), lambda i: (i, 0)),
            pl.BlockSpec((POOL_HALO, d), lambda i: (jnp.maximum(i * hb - 1, 0), 0)),
            pl.BlockSpec((None, 1, d), lambda i: (layer, 0, 0)),
            pl.BlockSpec((None, n_g, group, group), lambda i: (layer, 0, 0, 0)),
            pl.BlockSpec((None, 1, d), lambda i: (layer, 0, 0)),
        ],
        out_specs=pl.BlockSpec((tm, d), lambda i: (i, 0)),
        out_shape=jax.ShapeDtypeStruct((s, d), jnp.float32),
        compiler_params=_params("parallel"),
        name="pool",
    )(x, x, gain, w_pool, scale)


def _proj_kernel(*refs, n_w, head_norm):
    x_ref, g_ref, hg_ref = refs[0], refs[1], refs[2]
    w_refs = refs[3:3 + n_w]
    o_refs = refs[3 + n_w:3 + 2 * n_w]
    h_ref = refs[3 + 2 * n_w]

    @pl.when(pl.program_id(1) == 0)
    def _():
        h_ref[...] = _rms(x_ref[...], g_ref[...]).astype(h_ref.dtype)

    h = h_ref[...]
    for w_ref, o_ref, hn in zip(w_refs, o_refs, head_norm):
        acc = jnp.dot(h, w_ref[...], preferred_element_type=jnp.float32)
        if hn:
            hg = hg_ref[...]
            acc = jnp.concatenate(
                [_rms(acc[:, c:c + HEAD_DIM], hg) for c in range(0, acc.shape[1], HEAD_DIM)], axis=1)
        o_ref[...] = acc.astype(o_ref.dtype)


def _proj(x, gain, layer, head_gain, weights, w_layer, head_norm):
    s, d = x.shape
    n = weights[0].shape[-1]
    tm, tn = PROJ_TM, PROJ_TN
    n_w = len(weights)
    kern = functools.partial(_proj_kernel, n_w=n_w, head_norm=tuple(head_norm))
    w_spec = pl.BlockSpec((None, d, tn), lambda i, j: (w_layer, 0, j))
    o_spec = pl.BlockSpec((tm, tn), lambda i, j: (i, j))
    return pl.pallas_call(
        kern,
        grid=(s // tm, n // tn),
        in_specs=[
            pl.BlockSpec((tm, d), lambda i, j: (i, 0)),
            pl.BlockSpec((None, 1, d), lambda i, j: (layer, 0, 0)),
            pl.BlockSpec((1, HEAD_DIM), lambda i, j: (0, 0)),
        ] + [w_spec] * n_w,
        out_specs=[o_spec] * n_w,
        out_shape=[jax.ShapeDtypeStruct((s, n), MXU_DTYPE)] * n_w,
        scratch_shapes=[pltpu.VMEM((tm, d), MXU_DTYPE)],
        compiler_params=_params("parallel", "arbitrary"),
        name="proj",
    )(x, gain, head_gain, *weights)


def _attn_kernel(*refs, n_blk, tq, scale):
    q_ref = refs[0]
    k_refs = refs[1:1 + n_blk]
    v_refs = refs[1 + n_blk:1 + 2 * n_blk]
    b_ref = refs[1 + 2 * n_blk]
    o_ref = refs[2 + 2 * n_blk]
    i = pl.program_id(1)
    q = q_ref[...]
    k = jnp.concatenate([r[...] for r in k_refs], axis=0)
    v = jnp.concatenate([r[...] for r in v_refs], axis=0)
    s = lax.dot_general(q, k, (((1,), (1,)), ((), ())), preferred_element_type=jnp.float32)
    s = s * scale + b_ref[...]
    col = lax.broadcasted_iota(jnp.int32, s.shape, 1)
    s = jnp.where(col >= (n_blk - 1 - i) * tq, s, NEG_INF)
    m = jnp.max(s, axis=-1, keepdims=True)
    p = jnp.exp(s - m)
    l = jnp.sum(p, axis=-1, keepdims=True)
    o = jnp.dot(p.astype(v.dtype), v, preferred_element_type=jnp.float32) / l
    o_ref[...] = o.astype(o_ref.dtype)


def _band_bias(rel_table, tq, n_blk):
    r = jnp.arange(tq)[:, None]
    m = jnp.arange(n_blk * tq)[None, :]
    rel = r - m + (n_blk - 1) * tq
    idx = jnp.clip(rel, -(CHUNK - 1), REL_MAX) + (CHUNK - 1)
    cq = r // CHUNK + (n_blk - 1) * (tq // CHUNK)
    ck = m // CHUNK
    band = (ck <= cq) & (ck >= cq - LEFT_CHUNKS)
    bias = rel_table.astype(jnp.float32)[:, idx]
    return jnp.where(band[None], bias, NEG_INF)


def _attn(q, k, v, bias):
    s, d = q.shape
    n_heads = d // HEAD_DIM
    tq = ATTN_TQ
    n_blk = LEFT // tq + 1
    kern = functools.partial(_attn_kernel, n_blk=n_blk, tq=tq, scale=HEAD_DIM ** -0.5)
    q_spec = pl.BlockSpec((tq, HEAD_DIM), lambda h, i: (i, h))

    def kv_spec(b):
        return pl.BlockSpec((tq, HEAD_DIM), lambda h, i: (jnp.maximum(i - (n_blk - 1) + b, 0), h))

    kv_specs = [kv_spec(b) for b in range(n_blk)]
    return pl.pallas_call(
        kern,
        grid=(n_heads, s // tq),
        in_specs=[q_spec] + kv_specs + kv_specs + [
            pl.BlockSpec((None, tq, n_blk * tq), lambda h, i: (h, 0, 0))],
        out_specs=q_spec,
        out_shape=jax.ShapeDtypeStruct((s, d), MXU_DTYPE),
        compiler_params=_params("parallel", "arbitrary"),
        name="attn",
    )(q, *([k] * n_blk), *([v] * n_blk), bias)


def _oproj_kernel(a_ref, w_ref, x_ref, o_ref):
    o_ref[...] = x_ref[...] + jnp.dot(a_ref[...], w_ref[...], preferred_element_type=jnp.float32)


def _oproj(a, w, w_layer, x):
    s, d = x.shape
    tm, tn = PROJ_TM, PROJ_TN
    return pl.pallas_call(
        _oproj_kernel,
        grid=(s // tm, d // tn),
        in_specs=[
            pl.BlockSpec((tm, a.shape[1]), lambda i, j: (i, 0)),
            pl.BlockSpec((None, a.shape[1], tn), lambda i, j: (w_layer, 0, j)),
            pl.BlockSpec((tm, tn), lambda i, j: (i, j)),
        ],
        out_specs=pl.BlockSpec((tm, tn), lambda i, j: (i, j)),
        out_shape=jax.ShapeDtypeStruct((s, d), jnp.float32),
        compiler_params=_params("parallel", "arbitrary"),
        name="oproj",
    )(a, w, x)


def kernel(x, ffn1_norm, ffn1_w_gate, ffn1_w_up, ffn1_w_down, mix_norm, ffn2_norm, ffn2_w_gate, ffn2_w_up, ffn2_w_down, pool_w, pool_scale, kv_norm, w_k, w_v, k_gain, w_q, q_gain, rel_bias, w_o):
    b, s, d = x.shape
    depth = ffn1_norm.shape[0]
    n_a = pool_w.shape[0]
    cast = lambda w: w.astype(MXU_DTYPE)
    row = lambda g: g.reshape(g.shape[0], 1, g.shape[1])
    f1 = (row(ffn1_norm), cast(ffn1_w_gate), cast(ffn1_w_up), cast(ffn1_w_down))
    f2 = (row(ffn2_norm), cast(ffn2_w_gate), cast(ffn2_w_up), cast(ffn2_w_down))
    mix_g = row(mix_norm)
    pool_wc, pool_sc = cast(pool_w), row(pool_scale)
    wk, wv, wq, wo = cast(w_k)[None], cast(w_v)[None], cast(w_q), cast(w_o)
    kv_g = kv_norm.reshape(1, 1, d)
    n_blk = LEFT // ATTN_TQ + 1

    outs = []
    for bi in range(b):
        xs = x.reshape(b * s, d) if b == 1 else x[bi]
        k = v = None
        for l in range(depth):
            xs = _ffn(xs, *f1, l)
            if l < n_a:
                xs = _pool(xs, mix_g, pool_wc, pool_sc, l)
            else:
                a = l - n_a
                (q,) = _proj(xs, mix_g, l, q_gain[a].reshape(1, HEAD_DIM), (wq,), a, (True,))
                o = _attn(q, k, v, _band_bias(rel_bias[a], ATTN_TQ, n_blk))
                xs = _oproj(o, wo, a, xs)
            xs = _ffn(xs, *f2, l)
            if l == n_a - 1:
                k, v = _proj(xs, kv_g, 0, k_gain.reshape(1, HEAD_DIM), (wk, wv), 0, (True, False))
        outs.append(xs)
    return outs[0].reshape(b, s, d) if b == 1 else jnp.stack(outs, axis=0)
```

```python
import functools

import jax
import jax.numpy as jnp
from jax import lax
from jax.experimental import pallas as pl
from jax.experimental.pallas import tpu as pltpu

CHUNK = 64
POOL_WINDOWS = (2, 4, 8, 16)
HEAD_DIM = 128
LEFT_CHUNKS = 8
LEFT = LEFT_CHUNKS * CHUNK
REL_MAX = 128
EPS = 1e-6
NEG_INF = -1e30

MXU_DTYPE = jnp.bfloat16
POOL_HALO = 16

V7X_VMEM_BYTES = 64 * 1024 * 1024
VMEM_SLACK_BYTES = 8 * 1024 * 1024

FFN_TM = 1024
FFN_TF = 512
PROJ_TM = 512
PROJ_TN = 512
ATTN_TQ = 256
POOL_TM = 256


def _rms(x, g):
    ms = jnp.mean(x * x, axis=-1, keepdims=True)
    return (x * lax.rsqrt(ms + EPS)) * g


def _nbytes(shape, dtype):
    n = jnp.dtype(dtype).itemsize
    for s in shape:
        n *= s
    return n


def _params(sem, window_bytes):
    limit = min(window_bytes + VMEM_SLACK_BYTES, V7X_VMEM_BYTES - 2 * 1024 * 1024)
    return pltpu.CompilerParams(dimension_semantics=sem, vmem_limit_bytes=limit)


def _resident(block_shape, index_map):
    return pl.BlockSpec(block_shape, index_map, pipeline_mode=pl.Buffered(1))


def _ffn_kernel(x_ref, g_ref, wg_ref, wu_ref, wd_ref, o_ref, h_ref, *, n_f, tf, rem):
    j = pl.program_id(1)

    @pl.when(j == 0)
    def _():
        h_ref[...] = _rms(x_ref[...], g_ref[...]).astype(h_ref.dtype)
        o_ref[...] = jnp.zeros_like(o_ref)

    def step(width):
        h = h_ref[...]
        gate = jnp.dot(h, wg_ref[:, :width], preferred_element_type=jnp.float32)
        up = jnp.dot(h, wu_ref[:, :width], preferred_element_type=jnp.float32)
        a = (jax.nn.silu(gate) * up).astype(h.dtype)
        o_ref[...] += jnp.dot(a, wd_ref[:width, :], preferred_element_type=jnp.float32)

    @pl.when(j < n_f - 1)
    def _():
        step(tf)

    @pl.when(j == n_f - 1)
    def _():
        step(rem)
        o_ref[...] = x_ref[...] + 0.5 * o_ref[...]


def _ffn(x, gain, wg, wu, wd, layer):
    s, d = x.shape
    f = wg.shape[-1]
    tm, tf = FFN_TM, FFN_TF
    n_f = pl.cdiv(f, tf)
    rem = f - (n_f - 1) * tf
    kern = functools.partial(_ffn_kernel, n_f=n_f, tf=tf, rem=rem)
    vmem = (4 * _nbytes((tm, d), jnp.float32) + 6 * _nbytes((d, tf), MXU_DTYPE)
            + _nbytes((tm, d), MXU_DTYPE))
    return pl.pallas_call(
        kern,
        grid=(s // tm, n_f),
        in_specs=[
            pl.BlockSpec((tm, d), lambda i, j: (i, 0)),
            pl.BlockSpec((None, 1, d), lambda i, j: (layer, 0, 0)),
            pl.BlockSpec((None, d, tf), lambda i, j: (layer, 0, j)),
            pl.BlockSpec((None, d, tf), lambda i, j: (layer, 0, j)),
            pl.BlockSpec((None, tf, d), lambda i, j: (layer, j, 0)),
        ],
        out_specs=pl.BlockSpec((tm, d), lambda i, j: (i, 0)),
        out_shape=jax.ShapeDtypeStruct((s, d), jnp.float32),
        scratch_shapes=[pltpu.VMEM((tm, d), MXU_DTYPE)],
        compiler_params=_params(("parallel", "arbitrary"), vmem),
        name="ffn",
    )(x, gain, wg, wu, wd)


def _pool_kernel(x_ref, halo_ref, g_ref, w_ref, sc_ref, o_ref, *, tm, group):
    i = pl.program_id(0)
    x = x_ref[...]
    g = g_ref[...]
    h = _rms(x, g)
    hh = _rms(halo_ref[...], g)
    hh = jnp.where(i == 0, 0.0, hh)
    ext = jnp.concatenate([hh, h], axis=0)
    t = i * tm + lax.broadcasted_iota(jnp.int32, (tm, 1), 0)
    for gi, w in enumerate(POOL_WINDOWS):
        cols = slice(gi * group, (gi + 1) * group)
        acc = ext[:, cols]
        k = 1
        while k < w:
            acc = acc + pltpu.roll(acc, k, axis=0)
            k *= 2
        cnt = jnp.minimum(t + 1, w).astype(jnp.float32)
        pooled = acc[POOL_HALO:, :] / cnt
        diff = (pooled - h[:, cols]).astype(w_ref.dtype)
        y = jnp.dot(diff, w_ref[gi], preferred_element_type=jnp.float32)
        o_ref[:, cols] = x[:, cols] + y * sc_ref[:, cols]


def _pool(x, gain, w_pool, scale, layer):
    s, d = x.shape
    n_g, group = w_pool.shape[1], w_pool.shape[2]
    tm = POOL_TM
    hb = tm // POOL_HALO
    kern = functools.partial(_pool_kernel, tm=tm, group=group)
    vmem = (4 * _nbytes((tm, d), jnp.float32) + 2 * _nbytes((POOL_HALO, d), jnp.float32)
            + 2 * _nbytes(w_pool.shape[1:], MXU_DTYPE) + 6 * _nbytes((tm + POOL_HALO, d), jnp.float32))
    return pl.pallas_call(
        kern,
        grid=(s // tm,),
        in_specs=[
            pl.BlockSpec((tm, d), lambda i: (i, 0)),
            pl.BlockSpec((POOL_HALO, d), lambda i: (jnp.maximum(i * hb - 1, 0), 0)),
            pl.BlockSpec((None, 1, d), lambda i: (layer, 0, 0)),
            pl.BlockSpec((None, n_g, group, group), lambda i: (layer, 0, 0, 0)),
            pl.BlockSpec((None, 1, d), lambda i: (layer, 0, 0)),
        ],
        out_specs=pl.BlockSpec((tm, d), lambda i: (i, 0)),
        out_shape=jax.ShapeDtypeStruct((s, d), jnp.float32),
        compiler_params=_params(("parallel",), vmem),
        name="pool",
    )(x, x, gain, w_pool, scale)


def _proj_kernel(*refs, n_w, head_norm, out_scale, tn):
    x_ref, g_ref, hg_ref = refs[0], refs[1], refs[2]
    w_refs = refs[3:3 + n_w]
    o_refs = refs[3 + n_w:3 + 2 * n_w]
    h = _rms(x_ref[...], g_ref[...]).astype(w_refs[0].dtype)
    hg = hg_ref[...] * out_scale
    for w_ref, o_ref, hn in zip(w_refs, o_refs, head_norm):
        for c in range(0, w_ref.shape[1], tn):
            acc = jnp.dot(h, w_ref[:, c:c + tn], preferred_element_type=jnp.float32)
            if hn:
                acc = jnp.concatenate(
                    [_rms(acc[:, e:e + HEAD_DIM], hg) for e in range(0, tn, HEAD_DIM)], axis=1)
            o_ref[:, c:c + tn] = acc.astype(o_ref.dtype)


def _proj(x, gain, layer, head_gain, weights, w_layer, head_norm, out_scale=1.0):
    s, d = x.shape
    n = weights[0].shape[-1]
    tm = PROJ_TM
    n_w = len(weights)
    kern = functools.partial(_proj_kernel, n_w=n_w, head_norm=tuple(head_norm), out_scale=out_scale, tn=PROJ_TN)
    w_spec = _resident((None, d, n), lambda i: (w_layer, 0, 0))
    o_spec = pl.BlockSpec((tm, n), lambda i: (i, 0))
    vmem = (2 * _nbytes((tm, d), jnp.float32) + n_w * _nbytes((d, n), MXU_DTYPE)
            + 2 * n_w * _nbytes((tm, n), MXU_DTYPE) + _nbytes((tm, d), MXU_DTYPE))
    return pl.pallas_call(
        kern,
        grid=(s // tm,),
        in_specs=[
            pl.BlockSpec((tm, d), lambda i: (i, 0)),
            pl.BlockSpec((None, 1, d), lambda i: (layer, 0, 0)),
            pl.BlockSpec((1, HEAD_DIM), lambda i: (0, 0)),
        ] + [w_spec] * n_w,
        out_specs=[o_spec] * n_w,
        out_shape=[jax.ShapeDtypeStruct((s, n), MXU_DTYPE)] * n_w,
        compiler_params=_params(("parallel",), vmem),
        name="proj",
    )(x, gain, head_gain, *weights)


def _attn_kernel(*refs, n_blk, tq, n_heads):
    q_ref = refs[0]
    k_refs = refs[1:1 + n_blk]
    v_refs = refs[1 + n_blk:1 + 2 * n_blk]
    u_ref = refs[1 + 2 * n_blk]
    o_ref = refs[2 + 2 * n_blk]
    bias_ref = refs[3 + 2 * n_blk]
    i = pl.program_id(0)
    cpb = tq // CHUNK

    @pl.when(i == 0)
    def _():
        rq = lax.broadcasted_iota(jnp.int32, (tq, tq), 0) // CHUNK + (n_blk - 1) * cpb
        cc = lax.broadcasted_iota(jnp.int32, (tq, tq), 1) // CHUNK
        for h in range(n_heads):
            ub = jnp.broadcast_to(u_ref[h], (tq, u_ref.shape[-1]))
            t = pltpu.roll(ub, 0, axis=1, stride=1, stride_axis=0)
            for b in range(n_blk):
                ck = cc + b * cpb
                band = (ck <= rq) & (ck >= rq - LEFT_CHUNKS)
                bias_ref[h, b] = jnp.where(band, t[:, b * tq:(b + 1) * tq], NEG_INF)
            bias_ref[h, n_blk] = jnp.full((tq, tq), NEG_INF, jnp.float32)

    bias_idx = [jnp.where(i - (n_blk - 1) + b >= 0, b, n_blk) for b in range(n_blk - 1)] + [n_blk - 1]
    nt = (((1,), (1,)), ((), ()))
    for h in range(n_heads):
        hs = slice(h * HEAD_DIM, (h + 1) * HEAD_DIM)
        q = q_ref[:, hs]
        s = [lax.dot_general(q, k_refs[b][:, hs], nt, preferred_element_type=jnp.float32)
             + bias_ref[h, bias_idx[b]] for b in range(n_blk)]
        m = functools.reduce(jnp.maximum, s)
        m = jnp.max(m, axis=-1, keepdims=True)
        p = [jnp.exp(sb - m) for sb in s]
        l = jnp.sum(functools.reduce(lambda a, c: a + c, p), axis=-1, keepdims=True)
        o = functools.reduce(
            lambda a, c: a + c,
            [jnp.dot(p[b].astype(v_refs[b].dtype), v_refs[b][:, hs], preferred_element_type=jnp.float32)
             for b in range(n_blk)])
        o_ref[:, hs] = (o * (1.0 / l)).astype(o_ref.dtype)


def _bias_rows(rel_table, tq, width):
    n_rel = rel_table.shape[-1]
    far = LEFT - REL_MAX
    t = rel_table.astype(jnp.float32)
    oldest = jnp.broadcast_to(t[:, n_rel - 1:], (t.shape[0], far))
    newest = jnp.broadcast_to(t[:, :1], (t.shape[0], width - tq - far - n_rel))
    wrapped = jnp.broadcast_to(t[:, n_rel - 1:], (t.shape[0], tq))
    u = jnp.concatenate([oldest, t[:, ::-1], newest, wrapped], axis=1)
    return u.reshape(t.shape[0], 1, width)


def _attn(q, k, v, rel_table):
    s, d = q.shape
    n_heads = d // HEAD_DIM
    tq = ATTN_TQ
    n_blk = LEFT // tq + 1
    width = (n_blk + 1) * tq
    u = _bias_rows(rel_table, tq, width)
    kern = functools.partial(_attn_kernel, n_blk=n_blk, tq=tq, n_heads=n_heads)
    q_spec = pl.BlockSpec((tq, d), lambda i: (i, 0))

    def kv_spec(b):
        return pl.BlockSpec((tq, d), lambda i: (jnp.maximum(i - (n_blk - 1) + b, 0), 0))

    kv_specs = [kv_spec(b) for b in range(n_blk)]
    bias_shape = (n_heads, n_blk + 1, tq, tq)
    score_bytes = n_heads * n_blk * _nbytes((tq, tq), jnp.float32)
    vmem = ((4 + 4 * n_blk) * _nbytes((tq, d), MXU_DTYPE) + _nbytes(bias_shape, jnp.float32)
            + _nbytes(u.shape, jnp.float32) + score_bytes)
    return pl.pallas_call(
        kern,
        grid=(s // tq,),
        in_specs=[q_spec] + kv_specs + kv_specs + [_resident(u.shape, lambda i: (0, 0, 0))],
        out_specs=q_spec,
        out_shape=jax.ShapeDtypeStruct((s, d), MXU_DTYPE),
        scratch_shapes=[pltpu.VMEM(bias_shape, jnp.float32)],
        compiler_params=_params(("arbitrary",), vmem),
        name="attn",
    )(q, *([k] * n_blk), *([v] * n_blk), u)


def _oproj_kernel(a_ref, w_ref, x_ref, o_ref, *, tn):
    a = a_ref[...]
    for c in range(0, w_ref.shape[1], tn):
        o_ref[:, c:c + tn] = x_ref[:, c:c + tn] + jnp.dot(
            a, w_ref[:, c:c + tn], preferred_element_type=jnp.float32)


def _oproj(a, w, w_layer, x):
    s, d = x.shape
    tm = PROJ_TM
    kd = a.shape[1]
    vmem = (2 * _nbytes((tm, kd), MXU_DTYPE) + _nbytes((kd, d), MXU_DTYPE) + 4 * _nbytes((tm, d), jnp.float32))
    return pl.pallas_call(
        functools.partial(_oproj_kernel, tn=PROJ_TN),
        grid=(s // tm,),
        in_specs=[
            pl.BlockSpec((tm, kd), lambda i: (i, 0)),
            _resident((None, kd, d), lambda i: (w_layer, 0, 0)),
            pl.BlockSpec((tm, d), lambda i: (i, 0)),
        ],
        out_specs=pl.BlockSpec((tm, d), lambda i: (i, 0)),
        out_shape=jax.ShapeDtypeStruct((s, d), jnp.float32),
        compiler_params=_params(("parallel",), vmem),
        name="oproj",
    )(a, w, x)


def kernel(x, ffn1_norm, ffn1_w_gate, ffn1_w_up, ffn1_w_down, mix_norm, ffn2_norm, ffn2_w_gate, ffn2_w_up, ffn2_w_down, pool_w, pool_scale, kv_norm, w_k, w_v, k_gain, w_q, q_gain, rel_bias, w_o):
    b, s, d = x.shape
    depth = ffn1_norm.shape[0]
    n_a = pool_w.shape[0]
    cast = lambda w: w.astype(MXU_DTYPE)
    row = lambda g: g.reshape(g.shape[0], 1, g.shape[1])
    f1 = (row(ffn1_norm), cast(ffn1_w_gate), cast(ffn1_w_up), cast(ffn1_w_down))
    f2 = (row(ffn2_norm), cast(ffn2_w_gate), cast(ffn2_w_up), cast(ffn2_w_down))
    mix_g = row(mix_norm)
    pool_wc, pool_sc = cast(pool_w), row(pool_scale)
    wk, wv, wq, wo = cast(w_k)[None], cast(w_v)[None], cast(w_q), cast(w_o)
    kv_g = kv_norm.reshape(1, 1, d)

    outs = []
    for bi in range(b):
        xs = x.reshape(b * s, d) if b == 1 else x[bi]
        k = v = None
        for l in range(depth):
            xs = _ffn(xs, *f1, l)
            if l < n_a:
                xs = _pool(xs, mix_g, pool_wc, pool_sc, l)
            else:
                a = l - n_a
                (q,) = _proj(xs, mix_g, l, q_gain[a].reshape(1, HEAD_DIM), (wq,), a, (True,),
                             out_scale=HEAD_DIM ** -0.5)
                o = _attn(q, k, v, rel_bias[a])
                xs = _oproj(o, wo, a, xs)
            xs = _ffn(xs, *f2, l)
            if l == n_a - 1:
                k, v = _proj(xs, kv_g, 0, k_gain.reshape(1, HEAD_DIM), (wk, wv), 0, (True, False))
        outs.append(xs)
    return outs[0].reshape(b, s, d) if b == 1 else jnp.stack(outs, axis=0)
```

```python
import functools

import jax
import jax.numpy as jnp
from jax import lax
from jax.experimental import pallas as pl
from jax.experimental.pallas import tpu as pltpu

CHUNK = 64
POOL_WINDOWS = (2, 4, 8, 16)
HEAD_DIM = 128
LEFT_CHUNKS = 8
LEFT = LEFT_CHUNKS * CHUNK
REL_MAX = 128
EPS = 1e-6
NEG_INF = -1e30

MXU_DTYPE = jnp.bfloat16
POOL_HALO = 16

V7X_VMEM_BYTES = 64 * 1024 * 1024
VMEM_SLACK_BYTES = 8 * 1024 * 1024

FFN_TM = 1024
FFN_TF = 512
FFN_CAST_TF = 256
PROJ_TM = 512
PROJ_TN = 512
ATTN_TQ = 256
POOL_TM = 256


def _rms(x, g):
    ms = jnp.mean(x * x, axis=-1, keepdims=True)
    return (x * lax.rsqrt(ms + EPS)) * g


def _nbytes(shape, dtype):
    n = jnp.dtype(dtype).itemsize
    for s in shape:
        n *= s
    return n


def _params(sem, window_bytes):
    limit = min(window_bytes + VMEM_SLACK_BYTES, V7X_VMEM_BYTES - 2 * 1024 * 1024)
    return pltpu.CompilerParams(dimension_semantics=sem, vmem_limit_bytes=limit)


def _resident(block_shape, index_map):
    return pl.BlockSpec(block_shape, index_map, pipeline_mode=pl.Buffered(1))


def _ffn_kernel(*refs, n_f, tf, rem, cast_out):
    x_ref, g_ref, wg_ref, wu_ref, wd_ref = refs[:5]
    if cast_out:
        o_ref, wgc_ref, wuc_ref, wdc_ref, h_ref = refs[5:]
    else:
        o_ref, h_ref = refs[6:]
    j = pl.program_id(1)

    @pl.when(j == 0)
    def _():
        h_ref[...] = _rms(x_ref[...], g_ref[...]).astype(h_ref.dtype)
        o_ref[...] = jnp.zeros_like(o_ref)

    def step(width):
        wg, wu, wd = wg_ref[:, :width], wu_ref[:, :width], wd_ref[:width, :]
        if cast_out:
            wg, wu, wd = wg.astype(MXU_DTYPE), wu.astype(MXU_DTYPE), wd.astype(MXU_DTYPE)
            wgc_ref[:, :width] = wg
            wuc_ref[:, :width] = wu
            wdc_ref[:width, :] = wd
        h = h_ref[...]
        gate = jnp.dot(h, wg, preferred_element_type=jnp.float32)
        up = jnp.dot(h, wu, preferred_element_type=jnp.float32)
        a = (jax.nn.silu(gate) * up).astype(h.dtype)
        o_ref[...] += jnp.dot(a, wd, preferred_element_type=jnp.float32)

    @pl.when(j < n_f - 1)
    def _():
        step(tf)

    @pl.when(j == n_f - 1)
    def _():
        step(rem)
        o_ref[...] = x_ref[...] + 0.5 * o_ref[...]


def _ffn_call(x, gain, layer, wg, wu, wd, w_layer, *, tile0, n_tiles, tf, partial_out=None):
    s, d = x.shape
    f = wg.shape[-1]
    tm = FFN_TM
    cast_out = partial_out is None
    n_f = pl.cdiv(f, tf)
    rem = f - (n_f - 1) * tf
    kern = functools.partial(_ffn_kernel, n_f=n_f, tf=tf, rem=rem, cast_out=cast_out)
    x_map = lambda i, j: (i + tile0, 0)
    x_spec = _resident((tm, d), x_map) if n_tiles == 1 else pl.BlockSpec((tm, d), x_map)
    in_specs = [
        x_spec,
        pl.BlockSpec((None, 1, d), lambda i, j: (layer, 0, 0)),
        pl.BlockSpec((None, d, tf), lambda i, j: (w_layer, 0, j)),
        pl.BlockSpec((None, d, tf), lambda i, j: (w_layer, 0, j)),
        pl.BlockSpec((None, tf, d), lambda i, j: (w_layer, j, 0)),
    ]
    out_specs = [pl.BlockSpec((tm, d), x_map)]
    out_shape = [jax.ShapeDtypeStruct((s, d), jnp.float32)]
    args = [x, gain, wg, wu, wd]
    x_bufs = 1 if n_tiles == 1 else 2
    vmem = ((x_bufs + 2) * _nbytes((tm, d), jnp.float32) + 6 * _nbytes((d, tf), wg.dtype)
            + _nbytes((tm, d), MXU_DTYPE))
    aliases = {}
    if cast_out:
        out_specs += [pl.BlockSpec((None, d, tf), lambda i, j: (0, 0, j)),
                      pl.BlockSpec((None, d, tf), lambda i, j: (0, 0, j)),
                      pl.BlockSpec((None, tf, d), lambda i, j: (0, j, 0))]
        out_shape += [jax.ShapeDtypeStruct((1, d, f), MXU_DTYPE), jax.ShapeDtypeStruct((1, d, f), MXU_DTYPE),
                      jax.ShapeDtypeStruct((1, f, d), MXU_DTYPE)]
        vmem += 6 * _nbytes((d, tf), MXU_DTYPE)
    else:
        in_specs.append(pl.BlockSpec(memory_space=pl.ANY))
        args.append(partial_out)
        aliases = {len(args) - 1: 0}
    return pl.pallas_call(
        kern,
        grid=(n_tiles, n_f),
        in_specs=in_specs,
        out_specs=out_specs,
        out_shape=out_shape,
        scratch_shapes=[pltpu.VMEM((tm, d), MXU_DTYPE)],
        input_output_aliases=aliases,
        compiler_params=_params(("arbitrary", "arbitrary"), vmem),
        name="ffn_cast" if cast_out else "ffn",
    )(*args)


def _ffn(x, gain, wg, wu, wd, layer):
    n_tiles = x.shape[0] // FFN_TM
    out, wgc, wuc, wdc = _ffn_call(x, gain, layer, wg, wu, wd, layer, tile0=0, n_tiles=1, tf=FFN_CAST_TF)
    if n_tiles == 1:
        return out
    (out,) = _ffn_call(x, gain, layer, wgc, wuc, wdc, 0, tile0=1, n_tiles=n_tiles - 1, tf=FFN_TF, partial_out=out)
    return out


def _pool_kernel(x_ref, halo_ref, g_ref, w_ref, sc_ref, o_ref, *, tm, group):
    i = pl.program_id(0)
    x = x_ref[...]
    g = g_ref[...]
    h = _rms(x, g)
    hh = _rms(halo_ref[...], g)
    hh = jnp.where(i == 0, 0.0, hh)
    ext = jnp.concatenate([hh, h], axis=0)
    t = i * tm + lax.broadcasted_iota(jnp.int32, (tm, 1), 0)
    for gi, w in enumerate(POOL_WINDOWS):
        cols = slice(gi * group, (gi + 1) * group)
        acc = ext[:, cols]
        k = 1
        while k < w:
            acc = acc + pltpu.roll(acc, k, axis=0)
            k *= 2
        cnt = jnp.minimum(t + 1, w).astype(jnp.float32)
        pooled = acc[POOL_HALO:, :] / cnt
        diff = (pooled - h[:, cols]).astype(w_ref.dtype)
        y = jnp.dot(diff, w_ref[gi], preferred_element_type=jnp.float32)
        o_ref[:, cols] = x[:, cols] + y * sc_ref[:, cols]


def _pool(x, gain, w_pool, scale, layer):
    s, d = x.shape
    n_g, group = w_pool.shape[1], w_pool.shape[2]
    tm = POOL_TM
    hb = tm // POOL_HALO
    kern = functools.partial(_pool_kernel, tm=tm, group=group)
    vmem = (4 * _nbytes((tm, d), jnp.float32) + 2 * _nbytes((POOL_HALO, d), jnp.float32)
            + 2 * _nbytes(w_pool.shape[1:], MXU_DTYPE) + 6 * _nbytes((tm + POOL_HALO, d), jnp.float32))
    return pl.pallas_call(
        kern,
        grid=(s // tm,),
        in_specs=[
            pl.BlockSpec((tm, d), lambda i: (i, 0)),
            pl.BlockSpec((POOL_HALO, d), lambda i: (jnp.maximum(i * hb - 1, 0), 0)),
            pl.BlockSpec((None, 1, d), lambda i: (layer, 0, 0)),
            pl.BlockSpec((None, n_g, group, group), lambda i: (layer, 0, 0, 0)),
            pl.BlockSpec((None, 1, d), lambda i: (layer, 0, 0)),
        ],
        out_specs=pl.BlockSpec((tm, d), lambda i: (i, 0)),
        out_shape=jax.ShapeDtypeStruct((s, d), jnp.float32),
        compiler_params=_params(("parallel",), vmem),
        name="pool",
    )(x, x, gain, w_pool, scale)


def _proj_kernel(*refs, n_w, head_norm, out_scale, tn):
    x_ref, g_ref, hg_ref = refs[0], refs[1], refs[2]
    w_refs = refs[3:3 + n_w]
    o_refs = refs[3 + n_w:3 + 2 * n_w]
    h = _rms(x_ref[...], g_ref[...]).astype(w_refs[0].dtype)
    hg = hg_ref[...] * out_scale
    for w_ref, o_ref, hn in zip(w_refs, o_refs, head_norm):
        for c in range(0, w_ref.shape[1], tn):
            acc = jnp.dot(h, w_ref[:, c:c + tn], preferred_element_type=jnp.float32)
            if hn:
                acc = jnp.concatenate(
                    [_rms(acc[:, e:e + HEAD_DIM], hg) for e in range(0, tn, HEAD_DIM)], axis=1)
            o_ref[:, c:c + tn] = acc.astype(o_ref.dtype)


def _proj(x, gain, layer, head_gain, weights, w_layer, head_norm, out_scale=1.0):
    s, d = x.shape
    n = weights[0].shape[-1]
    tm = PROJ_TM
    n_w = len(weights)
    kern = functools.partial(_proj_kernel, n_w=n_w, head_norm=tuple(head_norm), out_scale=out_scale, tn=PROJ_TN)
    w_spec = _resident((None, d, n), lambda i: (w_layer, 0, 0))
    o_spec = pl.BlockSpec((tm, n), lambda i: (i, 0))
    vmem = (2 * _nbytes((tm, d), jnp.float32) + n_w * _nbytes((d, n), MXU_DTYPE)
            + 2 * n_w * _nbytes((tm, n), MXU_DTYPE) + _nbytes((tm, d), MXU_DTYPE))
    return pl.pallas_call(
        kern,
        grid=(s // tm,),
        in_specs=[
            pl.BlockSpec((tm, d), lambda i: (i, 0)),
            pl.BlockSpec((None, 1, d), lambda i: (layer, 0, 0)),
            pl.BlockSpec((1, HEAD_DIM), lambda i: (0, 0)),
        ] + [w_spec] * n_w,
        out_specs=[o_spec] * n_w,
        out_shape=[jax.ShapeDtypeStruct((s, n), MXU_DTYPE)] * n_w,
        compiler_params=_params(("parallel",), vmem),
        name="proj",
    )(x, gain, head_gain, *weights)


def _attn_kernel(*refs, n_blk, tq, n_heads):
    q_ref = refs[0]
    k_refs = refs[1:1 + n_blk]
    v_refs = refs[1 + n_blk:1 + 2 * n_blk]
    u_ref = refs[1 + 2 * n_blk]
    o_ref = refs[2 + 2 * n_blk]
    bias_ref = refs[3 + 2 * n_blk]
    i = pl.program_id(0)
    cpb = tq // CHUNK

    @pl.when(i == 0)
    def _():
        rq = lax.broadcasted_iota(jnp.int32, (tq, tq), 0) // CHUNK + (n_blk - 1) * cpb
        cc = lax.broadcasted_iota(jnp.int32, (tq, tq), 1) // CHUNK
        for h in range(n_heads):
            ub = jnp.broadcast_to(u_ref[h], (tq, u_ref.shape[-1]))
            t = pltpu.roll(ub, 0, axis=1, stride=1, stride_axis=0)
            for b in range(n_blk):
                ck = cc + b * cpb
                band = (ck <= rq) & (ck >= rq - LEFT_CHUNKS)
                bias_ref[h, b] = jnp.where(band, t[:, b * tq:(b + 1) * tq], NEG_INF)
            bias_ref[h, n_blk] = jnp.full((tq, tq), NEG_INF, jnp.float32)

    bias_idx = [jnp.where(i - (n_blk - 1) + b >= 0, b, n_blk) for b in range(n_blk - 1)] + [n_blk - 1]
    nt = (((1,), (1,)), ((), ()))
    for h in range(n_heads):
        hs = slice(h * HEAD_DIM, (h + 1) * HEAD_DIM)
        q = q_ref[:, hs]
        s = [lax.dot_general(q, k_refs[b][:, hs], nt, preferred_element_type=jnp.float32)
             + bias_ref[h, bias_idx[b]] for b in range(n_blk)]
        m = functools.reduce(jnp.maximum, s)
        m = jnp.max(m, axis=-1, keepdims=True)
        p = [jnp.exp(sb - m) for sb in s]
        l = jnp.sum(functools.reduce(lambda a, c: a + c, p), axis=-1, keepdims=True)
        o = functools.reduce(
            lambda a, c: a + c,
            [jnp.dot(p[b].astype(v_refs[b].dtype), v_refs[b][:, hs], preferred_element_type=jnp.float32)
             for b in range(n_blk)])
        o_ref[:, hs] = (o * (1.0 / l)).astype(o_ref.dtype)


def _bias_rows(rel_table, tq, width):
    n_rel = rel_table.shape[-1]
    far = LEFT - REL_MAX
    t = rel_table.astype(jnp.float32)
    oldest = jnp.broadcast_to(t[:, n_rel - 1:], (t.shape[0], far))
    newest = jnp.broadcast_to(t[:, :1], (t.shape[0], width - tq - far - n_rel))
    wrapped = jnp.broadcast_to(t[:, n_rel - 1:], (t.shape[0], tq))
    u = jnp.concatenate([oldest, t[:, ::-1], newest, wrapped], axis=1)
    return u.reshape(t.shape[0], 1, width)


def _attn(q, k, v, rel_table):
    s, d = q.shape
    n_heads = d // HEAD_DIM
    tq = ATTN_TQ
    n_blk = LEFT // tq + 1
    width = (n_blk + 1) * tq
    u = _bias_rows(rel_table, tq, width)
    kern = functools.partial(_attn_kernel, n_blk=n_blk, tq=tq, n_heads=n_heads)
    q_spec = pl.BlockSpec((tq, d), lambda i: (i, 0))

    def kv_spec(b):
        return pl.BlockSpec((tq, d), lambda i: (jnp.maximum(i - (n_blk - 1) + b, 0), 0))

    kv_specs = [kv_spec(b) for b in range(n_blk)]
    bias_shape = (n_heads, n_blk + 1, tq, tq)
    score_bytes = n_heads * n_blk * _nbytes((tq, tq), jnp.float32)
    vmem = ((4 + 4 * n_blk) * _nbytes((tq, d), MXU_DTYPE) + _nbytes(bias_shape, jnp.float32)
            + _nbytes(u.shape, jnp.float32) + score_bytes)
    return pl.pallas_call(
        kern,
        grid=(s // tq,),
        in_specs=[q_spec] + kv_specs + kv_specs + [_resident(u.shape, lambda i: (0, 0, 0))],
        out_specs=q_spec,
        out_shape=jax.ShapeDtypeStruct((s, d), MXU_DTYPE),
        scratch_shapes=[pltpu.VMEM(bias_shape, jnp.float32)],
        compiler_params=_params(("arbitrary",), vmem),
        name="attn",
    )(q, *([k] * n_blk), *([v] * n_blk), u)


def _oproj_kernel(a_ref, w_ref, x_ref, o_ref, *, tn):
    a = a_ref[...]
    for c in range(0, w_ref.shape[1], tn):
        o_ref[:, c:c + tn] = x_ref[:, c:c + tn] + jnp.dot(
            a, w_ref[:, c:c + tn], preferred_element_type=jnp.float32)


def _oproj(a, w, w_layer, x):
    s, d = x.shape
    tm = PROJ_TM
    kd = a.shape[1]
    vmem = (2 * _nbytes((tm, kd), MXU_DTYPE) + _nbytes((kd, d), MXU_DTYPE) + 4 * _nbytes((tm, d), jnp.float32))
    return pl.pallas_call(
        functools.partial(_oproj_kernel, tn=PROJ_TN),
        grid=(s // tm,),
        in_specs=[
            pl.BlockSpec((tm, kd), lambda i: (i, 0)),
            _resident((None, kd, d), lambda i: (w_layer, 0, 0)),
            pl.BlockSpec((tm, d), lambda i: (i, 0)),
        ],
        out_specs=pl.BlockSpec((tm, d), lambda i: (i, 0)),
        out_shape=jax.ShapeDtypeStruct((s, d), jnp.float32),
        compiler_params=_params(("parallel",), vmem),
        name="oproj",
    )(a, w, x)


def kernel(x, ffn1_norm, ffn1_w_gate, ffn1_w_up, ffn1_w_down, mix_norm, ffn2_norm, ffn2_w_gate, ffn2_w_up, ffn2_w_down, pool_w, pool_scale, kv_norm, w_k, w_v, k_gain, w_q, q_gain, rel_bias, w_o):
    b, s, d = x.shape
    depth = ffn1_norm.shape[0]
    n_a = pool_w.shape[0]
    cast = lambda w: w.astype(MXU_DTYPE)
    row = lambda g: g.reshape(g.shape[0], 1, g.shape[1])
    f1 = (row(ffn1_norm), ffn1_w_gate, ffn1_w_up, ffn1_w_down)
    f2 = (row(ffn2_norm), ffn2_w_gate, ffn2_w_up, ffn2_w_down)
    mix_g = row(mix_norm)
    pool_wc, pool_sc = cast(pool_w), row(pool_scale)
    wk, wv, wq, wo = cast(w_k)[None], cast(w_v)[None], cast(w_q), cast(w_o)
    kv_g = kv_norm.reshape(1, 1, d)

    outs = []
    for bi in range(b):
        xs = x.reshape(b * s, d) if b == 1 else x[bi]
        k = v = None
        for l in range(depth):
            xs = _ffn(xs, *f1, l)
            if l < n_a:
                xs = _pool(xs, mix_g, pool_wc, pool_sc, l)
            else:
                a = l - n_a
                (q,) = _proj(xs, mix_g, l, q_gain[a].reshape(1, HEAD_DIM), (wq,), a, (True,),
                             out_scale=HEAD_DIM ** -0.5)
                o = _attn(q, k, v, rel_bias[a])
                xs = _oproj(o, wo, a, xs)
            xs = _ffn(xs, *f2, l)
            if l == n_a - 1:
                k, v = _proj(xs, kv_g, 0, k_gain.reshape(1, HEAD_DIM), (wk, wv), 0, (True, False))
        outs.append(xs)
    return outs[0].reshape(b, s, d) if b == 1 else jnp.stack(outs, axis=0)
```

```python
import functools

import jax
import jax.numpy as jnp
from jax import lax
from jax.experimental import pallas as pl
from jax.experimental.pallas import tpu as pltpu

CHUNK = 64
POOL_WINDOWS = (2, 4, 8, 16)
HEAD_DIM = 128
LEFT_CHUNKS = 8
LEFT = LEFT_CHUNKS * CHUNK
REL_MAX = 128
EPS = 1e-6
NEG_INF = -1e30

MXU_DTYPE = jnp.bfloat16
POOL_HALO = 16

V7X_VMEM_BYTES = 64 * 1024 * 1024
VMEM_SLACK_BYTES = 8 * 1024 * 1024

FFN_TM = 1024
FFN_TF = 512
FFN_CAST_TF = 256
FFN_ROW_CHUNK = 512
PROJ_TM = 512
PROJ_TN = 512
ATTN_TQ = 256
POOL_TM = 512


def _rms(x, g):
    ms = jnp.mean(x * x, axis=-1, keepdims=True)
    return (x * lax.rsqrt(ms + EPS)) * g


def _nbytes(shape, dtype):
    n = jnp.dtype(dtype).itemsize
    for s in shape:
        n *= s
    return n


def _params(sem, window_bytes):
    limit = min(window_bytes + VMEM_SLACK_BYTES, V7X_VMEM_BYTES - 2 * 1024 * 1024)
    return pltpu.CompilerParams(dimension_semantics=sem, vmem_limit_bytes=limit)


def _resident(block_shape, index_map):
    return pl.BlockSpec(block_shape, index_map, pipeline_mode=pl.Buffered(1))


def _ffn_kernel(*refs, n_f, tf, rem, cast_out):
    x_ref, g_ref, wg_ref, wu_ref, wd_ref = refs[:5]
    if cast_out:
        o_ref, wgc_ref, wuc_ref, wdc_ref, h_ref = refs[5:]
    else:
        o_ref, h_ref = refs[6:]
    j = pl.program_id(1)
    row_chunks = [slice(r, r + FFN_ROW_CHUNK) for r in range(0, x_ref.shape[0], FFN_ROW_CHUNK)]

    def weights(width):
        if not cast_out:
            return wg_ref[:, :width], wu_ref[:, :width], wd_ref[:width, :]
        wgc_ref[:, :width] = wg_ref[:, :width].astype(MXU_DTYPE)
        wuc_ref[:, :width] = wu_ref[:, :width].astype(MXU_DTYPE)
        wdc_ref[:width, :] = wd_ref[:width, :].astype(MXU_DTYPE)
        if width < tf:
            wgc_ref[:, width:] = jnp.zeros((wgc_ref.shape[0], tf - width), MXU_DTYPE)
            wuc_ref[:, width:] = jnp.zeros((wuc_ref.shape[0], tf - width), MXU_DTYPE)
        return wgc_ref[:, :width], wuc_ref[:, :width], wdc_ref[:width, :]

    def swiglu(h, wg, wu, wd):
        gate = jnp.dot(h, wg, preferred_element_type=jnp.float32)
        up = jnp.dot(h, wu, preferred_element_type=jnp.float32)
        a = (jax.nn.silu(gate) * up).astype(h.dtype)
        return jnp.dot(a, wd, preferred_element_type=jnp.float32)

    @pl.when(j == 0)
    def _():
        w = weights(tf)
        g = g_ref[...]
        for rows in row_chunks:
            h = _rms(x_ref[rows, :], g).astype(h_ref.dtype)
            h_ref[rows, :] = h
            o_ref[rows, :] = swiglu(h, *w)

    @pl.when((j > 0) & (j < n_f - 1))
    def _():
        o_ref[...] += swiglu(h_ref[...], *weights(tf))

    @pl.when(j == n_f - 1)
    def _():
        o_ref[...] = x_ref[...] + 0.5 * (o_ref[...] + swiglu(h_ref[...], *weights(rem)))


def _ffn_call(x, gain, layer, wg, wu, wd, *, f, tile0, n_tiles, tf, partial_out=None):
    s, d = x.shape
    tm = FFN_TM
    cast_out = partial_out is None
    n_f = pl.cdiv(f, tf)
    assert n_f >= 2, "the kernel's first and last hidden-column steps must be distinct"
    rem = f - (n_f - 1) * tf
    kern = functools.partial(_ffn_kernel, n_f=n_f, tf=tf, rem=rem, cast_out=cast_out)
    x_map = lambda i, j: (i + tile0, 0)
    x_spec = _resident((tm, d), x_map) if n_tiles == 1 else pl.BlockSpec((tm, d), x_map)
    if cast_out:
        gu_spec = pl.BlockSpec((None, d, tf), lambda i, j: (layer, 0, j))
        wd_spec = pl.BlockSpec((None, tf, d), lambda i, j: (layer, j, 0))
    else:
        assert tf == FFN_TF
        gu_spec = pl.BlockSpec((None, d, tf), lambda i, j: (j, 0, 0))
        wd_spec = pl.BlockSpec((tf, d), lambda i, j: (j, 0))
    in_specs = [x_spec, pl.BlockSpec((None, 1, d), lambda i, j: (layer, 0, 0)), gu_spec, gu_spec, wd_spec]
    out_specs = [pl.BlockSpec((tm, d), x_map)]
    out_shape = [jax.ShapeDtypeStruct((s, d), jnp.float32)]
    args = [x, gain, wg, wu, wd]
    x_bufs = 1 if n_tiles == 1 else 2
    vmem = ((x_bufs + 2) * _nbytes((tm, d), jnp.float32) + 6 * _nbytes((d, tf), wg.dtype)
            + _nbytes((tm, d), MXU_DTYPE))
    aliases = {}
    if cast_out:
        per_tile = FFN_TF // tf
        n_tile = pl.cdiv(f, FFN_TF)
        assert per_tile * tf == FFN_TF and n_f == n_tile * per_tile, "column blocks must tile the padded copies"
        gu_out = pl.BlockSpec((None, d, tf), lambda i, j: (j // per_tile, 0, j % per_tile))
        out_specs += [gu_out, gu_out, pl.BlockSpec((tf, d), lambda i, j: (j, 0))]
        out_shape += [jax.ShapeDtypeStruct((n_tile, d, FFN_TF), MXU_DTYPE)] * 2 + [
            jax.ShapeDtypeStruct((f, d), MXU_DTYPE)]
        vmem += 6 * _nbytes((d, tf), MXU_DTYPE)
    else:
        in_specs.append(pl.BlockSpec(memory_space=pl.ANY))
        args.append(partial_out)
        aliases = {len(args) - 1: 0}
    return pl.pallas_call(
        kern,
        grid=(n_tiles, n_f),
        in_specs=in_specs,
        out_specs=out_specs,
        out_shape=out_shape,
        scratch_shapes=[pltpu.VMEM((tm, d), MXU_DTYPE)],
        input_output_aliases=aliases,
        compiler_params=_params(("arbitrary", "arbitrary"), vmem),
        name="ffn_cast" if cast_out else "ffn",
    )(*args)


def _ffn(x, gain, wg, wu, wd, layer):
    n_tiles = x.shape[0] // FFN_TM
    f = wg.shape[-1]
    out, wgc, wuc, wdc = _ffn_call(x, gain, layer, wg, wu, wd, f=f, tile0=0, n_tiles=1, tf=FFN_CAST_TF)
    if n_tiles == 1:
        return out
    (out,) = _ffn_call(x, gain, layer, wgc, wuc, wdc, f=f, tile0=1, n_tiles=n_tiles - 1, tf=FFN_TF, partial_out=out)
    return out


def _pool_kernel(x_ref, halo_ref, g_ref, w_ref, sc_ref, o_ref, *, tm, group):
    i = pl.program_id(0)
    x = x_ref[...]
    g = g_ref[...]
    h = _rms(x, g)
    hh = _rms(halo_ref[...], g)
    hh = jnp.where(i == 0, 0.0, hh)
    ext = jnp.concatenate([hh, h], axis=0)
    t = i * tm + lax.broadcasted_iota(jnp.int32, (tm, 1), 0)
    for gi, w in enumerate(POOL_WINDOWS):
        cols = slice(gi * group, (gi + 1) * group)
        acc = ext[:, cols]
        k = 1
        while k < w:
            acc = acc + pltpu.roll(acc, k, axis=0)
            k *= 2
        cnt = jnp.minimum(t + 1, w).astype(jnp.float32)
        pooled = acc[POOL_HALO:, :] / cnt
        diff = (pooled - h[:, cols]).astype(w_ref.dtype)
        y = jnp.dot(diff, w_ref[gi], preferred_element_type=jnp.float32)
        o_ref[:, cols] = x[:, cols] + y * sc_ref[:, cols]


def _pool(x, gain, w_pool, scale, layer):
    s, d = x.shape
    n_g, group = w_pool.shape[1], w_pool.shape[2]
    tm = POOL_TM
    hb = tm // POOL_HALO
    kern = functools.partial(_pool_kernel, tm=tm, group=group)
    vmem = (4 * _nbytes((tm, d), jnp.float32) + 2 * _nbytes((POOL_HALO, d), jnp.float32)
            + 2 * _nbytes(w_pool.shape[1:], MXU_DTYPE) + 6 * _nbytes((tm + POOL_HALO, d), jnp.float32))
    return pl.pallas_call(
        kern,
        grid=(s // tm,),
        in_specs=[
            pl.BlockSpec((tm, d), lambda i: (i, 0)),
            pl.BlockSpec((POOL_HALO, d), lambda i: (jnp.maximum(i * hb - 1, 0), 0)),
            pl.BlockSpec((None, 1, d), lambda i: (layer, 0, 0)),
            pl.BlockSpec((None, n_g, group, group), lambda i: (layer, 0, 0, 0)),
            pl.BlockSpec((None, 1, d), lambda i: (layer, 0, 0)),
        ],
        out_specs=pl.BlockSpec((tm, d), lambda i: (i, 0)),
        out_shape=jax.ShapeDtypeStruct((s, d), jnp.float32),
        compiler_params=_params(("parallel",), vmem),
        name="pool",
    )(x, x, gain, w_pool, scale)


def _proj_kernel(*refs, n_w, head_norm, out_scale, tn):
    x_ref, g_ref, hg_ref = refs[0], refs[1], refs[2]
    w_refs = refs[3:3 + n_w]
    o_refs = refs[3 + n_w:3 + 2 * n_w]
    h = _rms(x_ref[...], g_ref[...]).astype(w_refs[0].dtype)
    hg = hg_ref[...] * out_scale
    for w_ref, o_ref, hn in zip(w_refs, o_refs, head_norm):
        for c in range(0, w_ref.shape[1], tn):
            acc = jnp.dot(h, w_ref[:, c:c + tn], preferred_element_type=jnp.float32)
            if hn:
                acc = jnp.concatenate(
                    [_rms(acc[:, e:e + HEAD_DIM], hg) for e in range(0, tn, HEAD_DIM)], axis=1)
            o_ref[:, c:c + tn] = acc.astype(o_ref.dtype)


def _proj(x, gain, layer, head_gain, weights, w_layer, head_norm, out_scale=1.0):
    s, d = x.shape
    n = weights[0].shape[-1]
    tm = PROJ_TM
    n_w = len(weights)
    kern = functools.partial(_proj_kernel, n_w=n_w, head_norm=tuple(head_norm), out_scale=out_scale, tn=PROJ_TN)
    w_spec = _resident((None, d, n), lambda i: (w_layer, 0, 0))
    o_spec = pl.BlockSpec((tm, n), lambda i: (i, 0))
    vmem = (2 * _nbytes((tm, d), jnp.float32) + n_w * _nbytes((d, n), MXU_DTYPE)
            + 2 * n_w * _nbytes((tm, n), MXU_DTYPE) + _nbytes((tm, d), MXU_DTYPE))
    return pl.pallas_call(
        kern,
        grid=(s // tm,),
        in_specs=[
            pl.BlockSpec((tm, d), lambda i: (i, 0)),
            pl.BlockSpec((None, 1, d), lambda i: (layer, 0, 0)),
            pl.BlockSpec((1, HEAD_DIM), lambda i: (0, 0)),
        ] + [w_spec] * n_w,
        out_specs=[o_spec] * n_w,
        out_shape=[jax.ShapeDtypeStruct((s, n), MXU_DTYPE)] * n_w,
        compiler_params=_params(("parallel",), vmem),
        name="proj",
    )(x, gain, head_gain, *weights)


def _attn_kernel(*refs, n_blk, tq, n_heads):
    q_ref = refs[0]
    k_refs = refs[1:1 + n_blk]
    v_refs = refs[1 + n_blk:1 + 2 * n_blk]
    u_ref = refs[1 + 2 * n_blk]
    o_ref = refs[2 + 2 * n_blk]
    bias_ref = refs[3 + 2 * n_blk]
    i = pl.program_id(0)
    cpb = tq // CHUNK

    @pl.when(i == 0)
    def _():
        rq = lax.broadcasted_iota(jnp.int32, (tq, tq), 0) // CHUNK + (n_blk - 1) * cpb
        cc = lax.broadcasted_iota(jnp.int32, (tq, tq), 1) // CHUNK
        for h in range(n_heads):
            ub = jnp.broadcast_to(u_ref[h], (tq, u_ref.shape[-1]))
            t = pltpu.roll(ub, 0, axis=1, stride=1, stride_axis=0)
            for b in range(n_blk):
                ck = cc + b * cpb
                band = (ck <= rq) & (ck >= rq - LEFT_CHUNKS)
                bias_ref[h, b] = jnp.where(band, t[:, b * tq:(b + 1) * tq], NEG_INF)
            bias_ref[h, n_blk] = jnp.full((tq, tq), NEG_INF, jnp.float32)

    bias_idx = [jnp.where(i - (n_blk - 1) + b >= 0, b, n_blk) for b in range(n_blk - 1)] + [n_blk - 1]
    nt = (((1,), (1,)), ((), ()))
    for h in range(n_heads):
        hs = slice(h * HEAD_DIM, (h + 1) * HEAD_DIM)
        q = q_ref[:, hs]
        s = [lax.dot_general(q, k_refs[b][:, hs], nt, preferred_element_type=jnp.float32)
             + bias_ref[h, bias_idx[b]] for b in range(n_blk)]
        m = functools.reduce(jnp.maximum, s)
        m = jnp.max(m, axis=-1, keepdims=True)
        p = [jnp.exp(sb - m) for sb in s]
        l = jnp.sum(functools.reduce(lambda a, c: a + c, p), axis=-1, keepdims=True)
        o = functools.reduce(
            lambda a, c: a + c,
            [jnp.dot(p[b].astype(v_refs[b].dtype), v_refs[b][:, hs], preferred_element_type=jnp.float32)
             for b in range(n_blk)])
        o_ref[:, hs] = (o * (1.0 / l)).astype(o_ref.dtype)


def _bias_rows(rel_table, tq, width):
    n_rel = rel_table.shape[-1]
    far = LEFT - REL_MAX
    t = rel_table.astype(jnp.float32)
    oldest = jnp.broadcast_to(t[:, n_rel - 1:], (t.shape[0], far))
    newest = jnp.broadcast_to(t[:, :1], (t.shape[0], width - tq - far - n_rel))
    wrapped = jnp.broadcast_to(t[:, n_rel - 1:], (t.shape[0], tq))
    u = jnp.concatenate([oldest, t[:, ::-1], newest, wrapped], axis=1)
    return u.reshape(t.shape[0], 1, width)


def _attn(q, k, v, rel_table):
    s, d = q.shape
    n_heads = d // HEAD_DIM
    tq = ATTN_TQ
    n_blk = LEFT // tq + 1
    width = (n_blk + 1) * tq
    u = _bias_rows(rel_table, tq, width)
    kern = functools.partial(_attn_kernel, n_blk=n_blk, tq=tq, n_heads=n_heads)
    q_spec = pl.BlockSpec((tq, d), lambda i: (i, 0))

    def kv_spec(b):
        return pl.BlockSpec((tq, d), lambda i: (jnp.maximum(i - (n_blk - 1) + b, 0), 0))

    kv_specs = [kv_spec(b) for b in range(n_blk)]
    bias_shape = (n_heads, n_blk + 1, tq, tq)
    score_bytes = n_heads * n_blk * _nbytes((tq, tq), jnp.float32)
    vmem = ((4 + 4 * n_blk) * _nbytes((tq, d), MXU_DTYPE) + _nbytes(bias_shape, jnp.float32)
            + _nbytes(u.shape, jnp.float32) + score_bytes)
    return pl.pallas_call(
        kern,
        grid=(s // tq,),
        in_specs=[q_spec] + kv_specs + kv_specs + [_resident(u.shape, lambda i: (0, 0, 0))],
        out_specs=q_spec,
        out_shape=jax.ShapeDtypeStruct((s, d), MXU_DTYPE),
        scratch_shapes=[pltpu.VMEM(bias_shape, jnp.float32)],
        compiler_params=_params(("arbitrary",), vmem),
        name="attn",
    )(q, *([k] * n_blk), *([v] * n_blk), u)


def _oproj_kernel(a_ref, w_ref, x_ref, o_ref, *, tn):
    a = a_ref[...]
    for c in range(0, w_ref.shape[1], tn):
        o_ref[:, c:c + tn] = x_ref[:, c:c + tn] + jnp.dot(
            a, w_ref[:, c:c + tn], preferred_element_type=jnp.float32)


def _oproj(a, w, w_layer, x):
    s, d = x.shape
    tm = PROJ_TM
    kd = a.shape[1]
    vmem = (2 * _nbytes((tm, kd), MXU_DTYPE) + _nbytes((kd, d), MXU_DTYPE) + 4 * _nbytes((tm, d), jnp.float32))
    return pl.pallas_call(
        functools.partial(_oproj_kernel, tn=PROJ_TN),
        grid=(s // tm,),
        in_specs=[
            pl.BlockSpec((tm, kd), lambda i: (i, 0)),
            _resident((None, kd, d), lambda i: (w_layer, 0, 0)),
            pl.BlockSpec((tm, d), lambda i: (i, 0)),
        ],
        out_specs=pl.BlockSpec((tm, d), lambda i: (i, 0)),
        out_shape=jax.ShapeDtypeStruct((s, d), jnp.float32),
        compiler_params=_params(("parallel",), vmem),
        name="oproj",
    )(a, w, x)


def kernel(x, ffn1_norm, ffn1_w_gate, ffn1_w_up, ffn1_w_down, mix_norm, ffn2_norm, ffn2_w_gate, ffn2_w_up, ffn2_w_down, pool_w, pool_scale, kv_norm, w_k, w_v, k_gain, w_q, q_gain, rel_bias, w_o):
    b, s, d = x.shape
    depth = ffn1_norm.shape[0]
    n_a = pool_w.shape[0]
    cast = lambda w: w.astype(MXU_DTYPE)
    row = lambda g: g.reshape(g.shape[0], 1, g.shape[1])
    f1 = (row(ffn1_norm), ffn1_w_gate, ffn1_w_up, ffn1_w_down)
    f2 = (row(ffn2_norm), ffn2_w_gate, ffn2_w_up, ffn2_w_down)
    mix_g = row(mix_norm)
    pool_wc, pool_sc = cast(pool_w), row(pool_scale)
    wk, wv, wq, wo = cast(w_k)[None], cast(w_v)[None], cast(w_q), cast(w_o)
    kv_g = kv_norm.reshape(1, 1, d)

    outs = []
    for bi in range(b):
        xs = x.reshape(b * s, d) if b == 1 else x[bi]
        k = v = None
        for l in range(depth):
            xs = _ffn(xs, *f1, l)
            if l < n_a:
                xs = _pool(xs, mix_g, pool_wc, pool_sc, l)
            else:
                a = l - n_a
                (q,) = _proj(xs, mix_g, l, q_gain[a].reshape(1, HEAD_DIM), (wq,), a, (True,),
                             out_scale=HEAD_DIM ** -0.5)
                o = _attn(q, k, v, rel_bias[a])
                xs = _oproj(o, wo, a, xs)
            xs = _ffn(xs, *f2, l)
            if l == n_a - 1:
                k, v = _proj(xs, kv_g, 0, k_gain.reshape(1, HEAD_DIM), (wk, wv), 0, (True, False))
        outs.append(xs)
    return outs[0].reshape(b, s, d) if b == 1 else jnp.stack(outs, axis=0)
```

```python
import functools

import jax
import jax.numpy as jnp
from jax import lax
from jax.experimental import pallas as pl
from jax.experimental.pallas import tpu as pltpu

CHUNK = 64
POOL_WINDOWS = (2, 4, 8, 16)
HEAD_DIM = 128
LEFT_CHUNKS = 8
LEFT = LEFT_CHUNKS * CHUNK
REL_MAX = 128
EPS = 1e-6
NEG_INF = -1e30

MXU_DTYPE = jnp.bfloat16
POOL_HALO = 16

V7X_VMEM_BYTES = 64 * 1024 * 1024
VMEM_SLACK_BYTES = 8 * 1024 * 1024

FFN_TM = 1024
FFN_TF = 512
FFN_CAST_TF = 256
FFN_ROW_CHUNK = 512
PROJ_TM = 512
PROJ_TN = 512
ATTN_TQ = 256
POOL_TM = 512


def _rms(x, g):
    ms = jnp.mean(x * x, axis=-1, keepdims=True)
    return (x * lax.rsqrt(ms + EPS)) * g


def _nbytes(shape, dtype):
    n = jnp.dtype(dtype).itemsize
    for s in shape:
        n *= s
    return n


def _params(sem, window_bytes):
    limit = min(window_bytes + VMEM_SLACK_BYTES, V7X_VMEM_BYTES - 2 * 1024 * 1024)
    return pltpu.CompilerParams(dimension_semantics=sem, vmem_limit_bytes=limit)


def _resident(block_shape, index_map):
    return pl.BlockSpec(block_shape, index_map, pipeline_mode=pl.Buffered(1))


def _ffn_kernel(*refs, n_f, tf, rem, cast_out):
    x_ref, g_ref, wg_ref, wu_ref, wd_ref = refs[:5]
    if cast_out:
        o_ref, wgc_ref, wuc_ref, wdc_ref, h_ref = refs[5:]
    else:
        o_ref, h_ref = refs[6:]
    j = pl.program_id(1)
    row_chunks = [slice(r, r + FFN_ROW_CHUNK) for r in range(0, x_ref.shape[0], FFN_ROW_CHUNK)]

    def weights(width):
        if not cast_out:
            return wg_ref[:, :width], wu_ref[:, :width], wd_ref[:width, :]
        wgc_ref[:, :width] = wg_ref[:, :width].astype(MXU_DTYPE)
        wuc_ref[:, :width] = wu_ref[:, :width].astype(MXU_DTYPE)
        wdc_ref[:width, :] = wd_ref[:width, :].astype(MXU_DTYPE)
        if width < tf:
            wgc_ref[:, width:] = jnp.zeros((wgc_ref.shape[0], tf - width), MXU_DTYPE)
            wuc_ref[:, width:] = jnp.zeros((wuc_ref.shape[0], tf - width), MXU_DTYPE)
        return wgc_ref[:, :width], wuc_ref[:, :width], wdc_ref[:width, :]

    def swiglu(h, wg, wu, wd):
        gate = jnp.dot(h, wg, preferred_element_type=jnp.float32)
        up = jnp.dot(h, wu, preferred_element_type=jnp.float32)
        a = (jax.nn.silu(gate) * up).astype(h.dtype)
        return jnp.dot(a, wd, preferred_element_type=jnp.float32)

    @pl.when(j == 0)
    def _():
        w = weights(tf)
        g = g_ref[...]
        for rows in row_chunks:
            h = _rms(x_ref[rows, :], g).astype(h_ref.dtype)
            h_ref[rows, :] = h
            o_ref[rows, :] = swiglu(h, *w)

    @pl.when((j > 0) & (j < n_f - 1))
    def _():
        o_ref[...] += swiglu(h_ref[...], *weights(tf))

    @pl.when(j == n_f - 1)
    def _():
        o_ref[...] = x_ref[...] + 0.5 * (o_ref[...] + swiglu(h_ref[...], *weights(rem)))


def _ffn_call(x, gain, layer, wg, wu, wd, *, f, tile0, n_tiles, tf, partial_out=None):
    s, d = x.shape
    tm = FFN_TM
    cast_out = partial_out is None
    n_f = pl.cdiv(f, tf)
    assert n_f >= 2, "the kernel's first and last hidden-column steps must be distinct"
    rem = f - (n_f - 1) * tf
    kern = functools.partial(_ffn_kernel, n_f=n_f, tf=tf, rem=rem, cast_out=cast_out)
    x_map = lambda i, j: (i + tile0, 0)
    x_spec = _resident((tm, d), x_map) if n_tiles == 1 else pl.BlockSpec((tm, d), x_map)
    if cast_out:
        gu_spec = pl.BlockSpec((None, d, tf), lambda i, j: (layer, 0, j))
        wd_spec = pl.BlockSpec((None, tf, d), lambda i, j: (layer, j, 0))
    else:
        assert tf == FFN_TF
        gu_spec = pl.BlockSpec((None, d, tf), lambda i, j: (j, 0, 0))
        wd_spec = pl.BlockSpec((tf, d), lambda i, j: (j, 0))
    in_specs = [x_spec, pl.BlockSpec((None, 1, d), lambda i, j: (layer, 0, 0)), gu_spec, gu_spec, wd_spec]
    out_specs = [pl.BlockSpec((tm, d), x_map)]
    out_shape = [jax.ShapeDtypeStruct((s, d), jnp.float32)]
    args = [x, gain, wg, wu, wd]
    x_bufs = 1 if n_tiles == 1 else 2
    vmem = ((x_bufs + 2) * _nbytes((tm, d), jnp.float32) + 6 * _nbytes((d, tf), wg.dtype)
            + _nbytes((tm, d), MXU_DTYPE))
    aliases = {}
    if cast_out:
        per_tile = FFN_TF // tf
        n_tile = pl.cdiv(f, FFN_TF)
        assert per_tile * tf == FFN_TF and n_f == n_tile * per_tile, "column blocks must tile the padded copies"
        gu_out = pl.BlockSpec((None, d, tf), lambda i, j: (j // per_tile, 0, j % per_tile))
        out_specs += [gu_out, gu_out, pl.BlockSpec((tf, d), lambda i, j: (j, 0))]
        out_shape += [jax.ShapeDtypeStruct((n_tile, d, FFN_TF), MXU_DTYPE)] * 2 + [
            jax.ShapeDtypeStruct((f, d), MXU_DTYPE)]
        vmem += 6 * _nbytes((d, tf), MXU_DTYPE)
    else:
        in_specs.append(pl.BlockSpec(memory_space=pl.ANY))
        args.append(partial_out)
        aliases = {len(args) - 1: 0}
    return pl.pallas_call(
        kern,
        grid=(n_tiles, n_f),
        in_specs=in_specs,
        out_specs=out_specs,
        out_shape=out_shape,
        scratch_shapes=[pltpu.VMEM((tm, d), MXU_DTYPE)],
        input_output_aliases=aliases,
        compiler_params=_params(("arbitrary", "arbitrary"), vmem),
        name="ffn_cast" if cast_out else "ffn",
    )(*args)


def _ffn(x, gain, wg, wu, wd, layer):
    n_tiles = x.shape[0] // FFN_TM
    f = wg.shape[-1]
    out, wgc, wuc, wdc = _ffn_call(x, gain, layer, wg, wu, wd, f=f, tile0=0, n_tiles=1, tf=FFN_CAST_TF)
    if n_tiles == 1:
        return out
    (out,) = _ffn_call(x, gain, layer, wgc, wuc, wdc, f=f, tile0=1, n_tiles=n_tiles - 1, tf=FFN_TF, partial_out=out)
    return out


def _pool_kernel(x_ref, halo_ref, g_ref, w_ref, sc_ref, o_ref, *, tm, group):
    i = pl.program_id(0)
    x = x_ref[...]
    g = g_ref[...]
    h = _rms(x, g)
    hh = _rms(halo_ref[...], g)
    hh = jnp.where(i == 0, 0.0, hh)
    ext = jnp.concatenate([hh, h], axis=0)
    t = i * tm + lax.broadcasted_iota(jnp.int32, (tm, 1), 0)
    for gi, w in enumerate(POOL_WINDOWS):
        cols = slice(gi * group, (gi + 1) * group)
        acc = ext[:, cols]
        k = 1
        while k < w:
            acc = acc + pltpu.roll(acc, k, axis=0)
            k *= 2
        cnt = jnp.minimum(t + 1, w).astype(jnp.float32)
        pooled = acc[POOL_HALO:, :] / cnt
        diff = (pooled - h[:, cols]).astype(w_ref.dtype)
        y = jnp.dot(diff, w_ref[gi], preferred_element_type=jnp.float32)
        o_ref[:, cols] = x[:, cols] + y * sc_ref[:, cols]


def _pool(x, gain, w_pool, scale, layer):
    s, d = x.shape
    n_g, group = w_pool.shape[1], w_pool.shape[2]
    tm = POOL_TM
    hb = tm // POOL_HALO
    kern = functools.partial(_pool_kernel, tm=tm, group=group)
    vmem = (4 * _nbytes((tm, d), jnp.float32) + 2 * _nbytes((POOL_HALO, d), jnp.float32)
            + 2 * _nbytes(w_pool.shape[1:], MXU_DTYPE) + 6 * _nbytes((tm + POOL_HALO, d), jnp.float32))
    return pl.pallas_call(
        kern,
        grid=(s // tm,),
        in_specs=[
            pl.BlockSpec((tm, d), lambda i: (i, 0)),
            pl.BlockSpec((POOL_HALO, d), lambda i: (jnp.maximum(i * hb - 1, 0), 0)),
            pl.BlockSpec((None, 1, d), lambda i: (layer, 0, 0)),
            pl.BlockSpec((None, n_g, group, group), lambda i: (layer, 0, 0, 0)),
            pl.BlockSpec((None, 1, d), lambda i: (layer, 0, 0)),
        ],
        out_specs=pl.BlockSpec((tm, d), lambda i: (i, 0)),
        out_shape=jax.ShapeDtypeStruct((s, d), jnp.float32),
        compiler_params=_params(("parallel",), vmem),
        name="pool",
    )(x, x, gain, w_pool, scale)


def _proj_kernel(*refs, n_w, head_norm, out_scale, tn):
    x_ref, g_ref, hg_ref = refs[0], refs[1], refs[2]
    w_refs = refs[3:3 + n_w]
    o_refs = refs[3 + n_w:3 + 2 * n_w]
    h = _rms(x_ref[...], g_ref[...]).astype(w_refs[0].dtype)
    hg = hg_ref[...] * out_scale
    for w_ref, o_ref, hn in zip(w_refs, o_refs, head_norm):
        for c in range(0, w_ref.shape[1], tn):
            acc = jnp.dot(h, w_ref[:, c:c + tn], preferred_element_type=jnp.float32)
            if hn:
                acc = jnp.concatenate(
                    [_rms(acc[:, e:e + HEAD_DIM], hg) for e in range(0, tn, HEAD_DIM)], axis=1)
            o_ref[:, c:c + tn] = acc.astype(o_ref.dtype)


def _proj(x, gain, layer, head_gain, weights, w_layer, head_norm, out_scale=1.0):
    s, d = x.shape
    n = weights[0].shape[-1]
    tm = PROJ_TM
    n_w = len(weights)
    kern = functools.partial(_proj_kernel, n_w=n_w, head_norm=tuple(head_norm), out_scale=out_scale, tn=PROJ_TN)
    w_spec = _resident((None, d, n), lambda i: (w_layer, 0, 0))
    o_spec = pl.BlockSpec((tm, n), lambda i: (i, 0))
    vmem = (2 * _nbytes((tm, d), jnp.float32) + n_w * _nbytes((d, n), MXU_DTYPE)
            + 2 * n_w * _nbytes((tm, n), MXU_DTYPE) + _nbytes((tm, d), MXU_DTYPE))
    return pl.pallas_call(
        kern,
        grid=(s // tm,),
        in_specs=[
            pl.BlockSpec((tm, d), lambda i: (i, 0)),
            pl.BlockSpec((None, 1, d), lambda i: (layer, 0, 0)),
            pl.BlockSpec((1, HEAD_DIM), lambda i: (0, 0)),
        ] + [w_spec] * n_w,
        out_specs=[o_spec] * n_w,
        out_shape=[jax.ShapeDtypeStruct((s, n), MXU_DTYPE)] * n_w,
        compiler_params=_params(("parallel",), vmem),
        name="proj",
    )(x, gain, head_gain, *weights)


def _attn_kernel(*refs, n_blk, tq, n_heads):
    q_ref = refs[0]
    k_refs = refs[1:1 + n_blk]
    v_refs = refs[1 + n_blk:1 + 2 * n_blk]
    u_ref = refs[1 + 2 * n_blk]
    o_ref = refs[2 + 2 * n_blk]
    bias_ref = refs[3 + 2 * n_blk]
    i = pl.program_id(0)
    cpb = tq // CHUNK

    @pl.when(i == 0)
    def _():
        rq = lax.broadcasted_iota(jnp.int32, (tq, tq), 0) // CHUNK + (n_blk - 1) * cpb
        cc = lax.broadcasted_iota(jnp.int32, (tq, tq), 1) // CHUNK
        for h in range(n_heads):
            ub = jnp.broadcast_to(u_ref[h], (tq, u_ref.shape[-1]))
            t = pltpu.roll(ub, 0, axis=1, stride=1, stride_axis=0)
            for b in range(n_blk):
                ck = cc + b * cpb
                band = (ck <= rq) & (ck >= rq - LEFT_CHUNKS)
                bias_ref[h, b] = jnp.where(band, t[:, b * tq:(b + 1) * tq], NEG_INF)
            bias_ref[h, n_blk] = jnp.full((tq, tq), NEG_INF, jnp.float32)

    bias_idx = [jnp.where(i - (n_blk - 1) + b >= 0, b, n_blk) for b in range(n_blk - 1)] + [n_blk - 1]
    nt = (((1,), (1,)), ((), ()))
    def head_cols(h):
        return slice(h * HEAD_DIM, (h + 1) * HEAD_DIM)

    def scores(h):
        q = q_ref[:, head_cols(h)]
        return [lax.dot_general(q, k_refs[b][:, head_cols(h)], nt, preferred_element_type=jnp.float32)
                + bias_ref[h, bias_idx[b]] for b in range(n_blk)]

    def softmax(s):
        m = functools.reduce(jnp.maximum, s)
        m = jnp.max(m, axis=-1, keepdims=True)
        return [jnp.exp(sb - m).astype(MXU_DTYPE) for sb in s]

    ones = jnp.ones((tq, HEAD_DIM), MXU_DTYPE)

    def weighted_values(h, p):
        ov = functools.reduce(
            lambda a, c: a + c,
            [jnp.dot(p[b], jnp.concatenate([v_refs[b][:, head_cols(h)], ones], axis=1),
                     preferred_element_type=jnp.float32) for b in range(n_blk)])
        o, l = ov[:, :HEAD_DIM], ov[:, HEAD_DIM:HEAD_DIM + 1]
        o_ref[:, head_cols(h)] = (o * (1.0 / l)).astype(o_ref.dtype)

    s_next = scores(0)
    for h in range(n_heads):
        s = s_next
        if h + 1 < n_heads:
            s_next = scores(h + 1)
        weighted_values(h, softmax(s))


def _bias_rows(rel_table, tq, width):
    n_rel = rel_table.shape[-1]
    far = LEFT - REL_MAX
    t = rel_table.astype(jnp.float32)
    oldest = jnp.broadcast_to(t[:, n_rel - 1:], (t.shape[0], far))
    newest = jnp.broadcast_to(t[:, :1], (t.shape[0], width - tq - far - n_rel))
    wrapped = jnp.broadcast_to(t[:, n_rel - 1:], (t.shape[0], tq))
    u = jnp.concatenate([oldest, t[:, ::-1], newest, wrapped], axis=1)
    return u.reshape(t.shape[0], 1, width)


def _attn(q, k, v, rel_table):
    s, d = q.shape
    n_heads = d // HEAD_DIM
    tq = ATTN_TQ
    n_blk = LEFT // tq + 1
    width = (n_blk + 1) * tq
    u = _bias_rows(rel_table, tq, width)
    kern = functools.partial(_attn_kernel, n_blk=n_blk, tq=tq, n_heads=n_heads)
    q_spec = pl.BlockSpec((tq, d), lambda i: (i, 0))

    def kv_spec(b):
        return pl.BlockSpec((tq, d), lambda i: (jnp.maximum(i - (n_blk - 1) + b, 0), 0))

    kv_specs = [kv_spec(b) for b in range(n_blk)]
    bias_shape = (n_heads, n_blk + 1, tq, tq)
    score_bytes = n_heads * n_blk * _nbytes((tq, tq), jnp.float32)
    vmem = ((4 + 4 * n_blk) * _nbytes((tq, d), MXU_DTYPE) + _nbytes(bias_shape, jnp.float32)
            + _nbytes(u.shape, jnp.float32) + score_bytes)
    return pl.pallas_call(
        kern,
        grid=(s // tq,),
        in_specs=[q_spec] + kv_specs + kv_specs + [_resident(u.shape, lambda i: (0, 0, 0))],
        out_specs=q_spec,
        out_shape=jax.ShapeDtypeStruct((s, d), MXU_DTYPE),
        scratch_shapes=[pltpu.VMEM(bias_shape, jnp.float32)],
        compiler_params=_params(("arbitrary",), vmem),
        name="attn",
    )(q, *([k] * n_blk), *([v] * n_blk), u)


def _oproj_kernel(a_ref, w_ref, x_ref, o_ref, *, tn):
    a = a_ref[...]
    for c in range(0, w_ref.shape[1], tn):
        o_ref[:, c:c + tn] = x_ref[:, c:c + tn] + jnp.dot(
            a, w_ref[:, c:c + tn], preferred_element_type=jnp.float32)


def _oproj(a, w, w_layer, x):
    s, d = x.shape
    tm = PROJ_TM
    kd = a.shape[1]
    vmem = (2 * _nbytes((tm, kd), MXU_DTYPE) + _nbytes((kd, d), MXU_DTYPE) + 4 * _nbytes((tm, d), jnp.float32))
    return pl.pallas_call(
        functools.partial(_oproj_kernel, tn=PROJ_TN),
        grid=(s // tm,),
        in_specs=[
            pl.BlockSpec((tm, kd), lambda i: (i, 0)),
            _resident((None, kd, d), lambda i: (w_layer, 0, 0)),
            pl.BlockSpec((tm, d), lambda i: (i, 0)),
        ],
        out_specs=pl.BlockSpec((tm, d), lambda i: (i, 0)),
        out_shape=jax.ShapeDtypeStruct((s, d), jnp.float32),
        compiler_params=_params(("parallel",), vmem),
        name="oproj",
    )(a, w, x)


def kernel(x, ffn1_norm, ffn1_w_gate, ffn1_w_up, ffn1_w_down, mix_norm, ffn2_norm, ffn2_w_gate, ffn2_w_up, ffn2_w_down, pool_w, pool_scale, kv_norm, w_k, w_v, k_gain, w_q, q_gain, rel_bias, w_o):
    b, s, d = x.shape
    depth = ffn1_norm.shape[0]
    n_a = pool_w.shape[0]
    cast = lambda w: w.astype(MXU_DTYPE)
    row = lambda g: g.reshape(g.shape[0], 1, g.shape[1])
    f1 = (row(ffn1_norm), ffn1_w_gate, ffn1_w_up, ffn1_w_down)
    f2 = (row(ffn2_norm), ffn2_w_gate, ffn2_w_up, ffn2_w_down)
    mix_g = row(mix_norm)
    pool_wc, pool_sc = cast(pool_w), row(pool_scale)
    wk, wv, wq, wo = cast(w_k)[None], cast(w_v)[None], cast(w_q), cast(w_o)
    kv_g = kv_norm.reshape(1, 1, d)

    outs = []
    for bi in range(b):
        xs = x.reshape(b * s, d) if b == 1 else x[bi]
        k = v = None
        for l in range(depth):
            xs = _ffn(xs, *f1, l)
            if l < n_a:
                xs = _pool(xs, mix_g, pool_wc, pool_sc, l)
            else:
                a = l - n_a
                (q,) = _proj(xs, mix_g, l, q_gain[a].reshape(1, HEAD_DIM), (wq,), a, (True,),
                             out_scale=HEAD_DIM ** -0.5)
                o = _attn(q, k, v, rel_bias[a])
                xs = _oproj(o, wo, a, xs)
            xs = _ffn(xs, *f2, l)
            if l == n_a - 1:
                k, v = _proj(xs, kv_g, 0, k_gain.reshape(1, HEAD_DIM), (wk, wv), 0, (True, False))
        outs.append(xs)
    return outs[0].reshape(b, s, d) if b == 1 else jnp.stack(outs, axis=0)
```

```python
import functools

import jax
import jax.numpy as jnp
from jax import lax
from jax.experimental import pallas as pl
from jax.experimental.pallas import tpu as pltpu

CHUNK = 64
POOL_WINDOWS = (2, 4, 8, 16)
HEAD_DIM = 128
LEFT_CHUNKS = 8
LEFT = LEFT_CHUNKS * CHUNK
REL_MAX = 128
EPS = 1e-6
NEG_INF = -1e30

MXU_DTYPE = jnp.bfloat16
POOL_HALO = 16

V7X_VMEM_BYTES = 64 * 1024 * 1024
VMEM_SLACK_BYTES = 8 * 1024 * 1024

FFN_TM = 1024
FFN_TF = 512
FFN_CAST_TF = 512
FFN_ROW_CHUNK = 512
PROJ_TM = 512
PROJ_TN = 512
ATTN_TQ = 256
POOL_TM = 512


def _rms(x, g):
    ms = jnp.mean(x * x, axis=-1, keepdims=True)
    return (x * lax.rsqrt(ms + EPS)) * g


def _nbytes(shape, dtype):
    n = jnp.dtype(dtype).itemsize
    for s in shape:
        n *= s
    return n


def _params(sem, window_bytes):
    limit = min(window_bytes + VMEM_SLACK_BYTES, V7X_VMEM_BYTES - 2 * 1024 * 1024)
    return pltpu.CompilerParams(dimension_semantics=sem, vmem_limit_bytes=limit)


def _resident(block_shape, index_map):
    return pl.BlockSpec(block_shape, index_map, pipeline_mode=pl.Buffered(1))


def _ffn_kernel(*refs, n_f, tf, rem, cast_out):
    x_ref, g_ref, wg_ref, wu_ref, wd_ref = refs[:5]
    if cast_out:
        o_ref, wgc_ref, wuc_ref, h_ref = refs[5:]
    else:
        o_ref, h_ref = refs[5:]
    j = pl.program_id(1)
    row_chunks = [slice(r, r + FFN_ROW_CHUNK) for r in range(0, x_ref.shape[0], FFN_ROW_CHUNK)]

    def weights(width):
        wd = wd_ref[:width, :].astype(MXU_DTYPE)
        if not cast_out:
            return wg_ref[:, :width], wu_ref[:, :width], wd
        wgc_ref[:, :width] = wg_ref[:, :width].astype(MXU_DTYPE)
        wuc_ref[:, :width] = wu_ref[:, :width].astype(MXU_DTYPE)
        if width < tf:
            wgc_ref[:, width:] = jnp.zeros((wgc_ref.shape[0], tf - width), MXU_DTYPE)
            wuc_ref[:, width:] = jnp.zeros((wuc_ref.shape[0], tf - width), MXU_DTYPE)
        return wgc_ref[:, :width], wuc_ref[:, :width], wd

    def swiglu(h, wg, wu, wd):
        gate = jnp.dot(h, wg, preferred_element_type=jnp.float32)
        up = jnp.dot(h, wu, preferred_element_type=jnp.float32)
        a = (jax.nn.silu(gate) * up).astype(h.dtype)
        return jnp.dot(a, wd, preferred_element_type=jnp.float32)

    @pl.when(j == 0)
    def _():
        w = weights(tf)
        g = g_ref[...]
        for rows in row_chunks:
            h = _rms(x_ref[rows, :], g).astype(h_ref.dtype)
            h_ref[rows, :] = h
            o_ref[rows, :] = swiglu(h, *w)

    @pl.when((j > 0) & (j < n_f - 1))
    def _():
        o_ref[...] += swiglu(h_ref[...], *weights(tf))

    @pl.when(j == n_f - 1)
    def _():
        o_ref[...] = x_ref[...] + 0.5 * (o_ref[...] + swiglu(h_ref[...], *weights(rem)))


def _ffn_call(x, gain, layer, wg, wu, wd, *, f, tile0, n_tiles, tf, cast_out):
    s, d = x.shape
    tm = FFN_TM
    n_f = pl.cdiv(f, tf)
    assert n_f >= 2, "the kernel's first and last hidden-column steps must be distinct"
    rem = f - (n_f - 1) * tf
    kern = functools.partial(_ffn_kernel, n_f=n_f, tf=tf, rem=rem, cast_out=cast_out)
    x_map = lambda i, j: (i + tile0, 0)
    x_spec = _resident((tm, d), x_map) if n_tiles == 1 else pl.BlockSpec((tm, d), x_map)
    if cast_out:
        gu_spec = pl.BlockSpec((None, d, tf), lambda i, j: (layer, 0, j))
    else:
        assert tf == FFN_TF
        gu_spec = pl.BlockSpec((None, d, tf), lambda i, j: (j, 0, 0))
    wd_spec = pl.BlockSpec((None, tf, d), lambda i, j: (layer, j, 0))
    in_specs = [x_spec, pl.BlockSpec((None, 1, d), lambda i, j: (layer, 0, 0)), gu_spec, gu_spec, wd_spec]
    out_specs = [pl.BlockSpec((tm, d), x_map)]
    out_shape = [jax.ShapeDtypeStruct((s, d), jnp.float32)]
    args = [x, gain, wg, wu, wd]
    x_bufs = 1 if n_tiles == 1 else 2
    vmem = ((x_bufs + 2) * _nbytes((tm, d), jnp.float32) + 4 * _nbytes((d, tf), wg.dtype)
            + 2 * _nbytes((tf, d), wd.dtype) + _nbytes((tf, d), MXU_DTYPE) + _nbytes((tm, d), MXU_DTYPE))
    if cast_out:
        per_tile = FFN_TF // tf
        n_tile = pl.cdiv(f, FFN_TF)
        assert per_tile * tf == FFN_TF and n_f == n_tile * per_tile, "column blocks must tile the padded copies"
        gu_out = pl.BlockSpec((None, d, tf), lambda i, j: (j // per_tile, 0, j % per_tile))
        out_specs += [gu_out, gu_out]
        out_shape += [jax.ShapeDtypeStruct((n_tile, d, FFN_TF), MXU_DTYPE)] * 2
        vmem += 4 * _nbytes((d, tf), MXU_DTYPE)
    return pl.pallas_call(
        kern,
        grid=(n_tiles, n_f),
        in_specs=in_specs,
        out_specs=out_specs,
        out_shape=out_shape,
        scratch_shapes=[pltpu.VMEM((tm, d), MXU_DTYPE)],
        input_output_aliases={0: 0},
        compiler_params=_params(("arbitrary", "arbitrary"), vmem),
        name="ffn_cast" if cast_out else "ffn",
    )(*args)


def _ffn(x, gain, wg, wu, wd, layer):
    n_tiles = x.shape[0] // FFN_TM
    f = wg.shape[-1]
    out, wgc, wuc = _ffn_call(x, gain, layer, wg, wu, wd, f=f, tile0=0, n_tiles=1, tf=FFN_CAST_TF, cast_out=True)
    if n_tiles == 1:
        return out
    (out,) = _ffn_call(out, gain, layer, wgc, wuc, wd, f=f, tile0=1, n_tiles=n_tiles - 1, tf=FFN_TF, cast_out=False)
    return out


def _pool_kernel(x_ref, halo_ref, g_ref, w_ref, sc_ref, o_ref, *, tm, group):
    i = pl.program_id(0)
    x = x_ref[...]
    g = g_ref[...]
    h = _rms(x, g)
    hh = _rms(halo_ref[...], g)
    hh = jnp.where(i == 0, 0.0, hh)
    ext = jnp.concatenate([hh, h], axis=0)
    t = i * tm + lax.broadcasted_iota(jnp.int32, (tm, 1), 0)
    for gi, w in enumerate(POOL_WINDOWS):
        cols = slice(gi * group, (gi + 1) * group)
        acc = ext[:, cols]
        k = 1
        while k < w:
            acc = acc + pltpu.roll(acc, k, axis=0)
            k *= 2
        cnt = jnp.minimum(t + 1, w).astype(jnp.float32)
        pooled = acc[POOL_HALO:, :] / cnt
        diff = (pooled - h[:, cols]).astype(w_ref.dtype)
        y = jnp.dot(diff, w_ref[gi], preferred_element_type=jnp.float32)
        o_ref[:, cols] = x[:, cols] + y * sc_ref[:, cols]


def _pool(x, gain, w_pool, scale, layer):
    s, d = x.shape
    n_g, group = w_pool.shape[1], w_pool.shape[2]
    tm = POOL_TM
    hb = tm // POOL_HALO
    kern = functools.partial(_pool_kernel, tm=tm, group=group)
    vmem = (4 * _nbytes((tm, d), jnp.float32) + 2 * _nbytes((POOL_HALO, d), jnp.float32)
            + 2 * _nbytes(w_pool.shape[1:], MXU_DTYPE) + 6 * _nbytes((tm + POOL_HALO, d), jnp.float32))
    return pl.pallas_call(
        kern,
        grid=(s // tm,),
        in_specs=[
            pl.BlockSpec((tm, d), lambda i: (i, 0)),
            pl.BlockSpec((POOL_HALO, d), lambda i: (jnp.maximum(i * hb - 1, 0), 0)),
            pl.BlockSpec((None, 1, d), lambda i: (layer, 0, 0)),
            pl.BlockSpec((None, n_g, group, group), lambda i: (layer, 0, 0, 0)),
            pl.BlockSpec((None, 1, d), lambda i: (layer, 0, 0)),
        ],
        out_specs=pl.BlockSpec((tm, d), lambda i: (i, 0)),
        out_shape=jax.ShapeDtypeStruct((s, d), jnp.float32),
        compiler_params=_params(("parallel",), vmem),
        name="pool",
    )(x, x, gain, w_pool, scale)


def _proj_kernel(*refs, n_w, head_norm, out_scale, tn):
    x_ref, g_ref, hg_ref = refs[0], refs[1], refs[2]
    w_refs = refs[3:3 + n_w]
    o_refs = refs[3 + n_w:3 + 2 * n_w]
    h = _rms(x_ref[...], g_ref[...]).astype(w_refs[0].dtype)
    hg = hg_ref[...] * out_scale
    for w_ref, o_ref, hn in zip(w_refs, o_refs, head_norm):
        for c in range(0, w_ref.shape[1], tn):
            acc = jnp.dot(h, w_ref[:, c:c + tn], preferred_element_type=jnp.float32)
            if hn:
                acc = jnp.concatenate(
                    [_rms(acc[:, e:e + HEAD_DIM], hg) for e in range(0, tn, HEAD_DIM)], axis=1)
            o_ref[:, c:c + tn] = acc.astype(o_ref.dtype)


def _proj(x, gain, layer, head_gain, weights, w_layer, head_norm, out_scale=1.0):
    s, d = x.shape
    n = weights[0].shape[-1]
    tm = PROJ_TM
    n_w = len(weights)
    kern = functools.partial(_proj_kernel, n_w=n_w, head_norm=tuple(head_norm), out_scale=out_scale, tn=PROJ_TN)
    w_spec = _resident((None, d, n), lambda i: (w_layer, 0, 0))
    o_spec = pl.BlockSpec((tm, n), lambda i: (i, 0))
    vmem = (2 * _nbytes((tm, d), jnp.float32) + n_w * _nbytes((d, n), MXU_DTYPE)
            + 2 * n_w * _nbytes((tm, n), MXU_DTYPE) + _nbytes((tm, d), MXU_DTYPE))
    return pl.pallas_call(
        kern,
        grid=(s // tm,),
        in_specs=[
            pl.BlockSpec((tm, d), lambda i: (i, 0)),
            pl.BlockSpec((None, 1, d), lambda i: (layer, 0, 0)),
            pl.BlockSpec((1, HEAD_DIM), lambda i: (0, 0)),
        ] + [w_spec] * n_w,
        out_specs=[o_spec] * n_w,
        out_shape=[jax.ShapeDtypeStruct((s, n), MXU_DTYPE)] * n_w,
        compiler_params=_params(("parallel",), vmem),
        name="proj",
    )(x, gain, head_gain, *weights)


def _attn_kernel(*refs, n_blk, tq, n_heads):
    q_ref = refs[0]
    k_refs = refs[1:1 + n_blk]
    v_refs = refs[1 + n_blk:1 + 2 * n_blk]
    u_ref = refs[1 + 2 * n_blk]
    o_ref = refs[2 + 2 * n_blk]
    bias_ref = refs[3 + 2 * n_blk]
    i = pl.program_id(0)
    cpb = tq // CHUNK

    @pl.when(i == 0)
    def _():
        rq = lax.broadcasted_iota(jnp.int32, (tq, tq), 0) // CHUNK + (n_blk - 1) * cpb
        cc = lax.broadcasted_iota(jnp.int32, (tq, tq), 1) // CHUNK
        for h in range(n_heads):
            ub = jnp.broadcast_to(u_ref[h], (tq, u_ref.shape[-1]))
            t = pltpu.roll(ub, 0, axis=1, stride=1, stride_axis=0)
            for b in range(n_blk):
                ck = cc + b * cpb
                band = (ck <= rq) & (ck >= rq - LEFT_CHUNKS)
                bias_ref[h, b] = jnp.where(band, t[:, b * tq:(b + 1) * tq], NEG_INF)
            bias_ref[h, n_blk] = jnp.full((tq, tq), NEG_INF, jnp.float32)

    bias_idx = [jnp.where(i - (n_blk - 1) + b >= 0, b, n_blk) for b in range(n_blk - 1)] + [n_blk - 1]
    nt = (((1,), (1,)), ((), ()))
    def head_cols(h):
        return slice(h * HEAD_DIM, (h + 1) * HEAD_DIM)

    def scores(h):
        q = q_ref[:, head_cols(h)]
        return [lax.dot_general(q, k_refs[b][:, head_cols(h)], nt, preferred_element_type=jnp.float32)
                + bias_ref[h, bias_idx[b]] for b in range(n_blk)]

    def softmax(s):
        m = functools.reduce(jnp.maximum, s)
        m = jnp.max(m, axis=-1, keepdims=True)
        return [jnp.exp(sb - m).astype(MXU_DTYPE) for sb in s]

    ones = jnp.ones((tq, HEAD_DIM), MXU_DTYPE)

    def weighted_values(h, p):
        ov = functools.reduce(
            lambda a, c: a + c,
            [jnp.dot(p[b], jnp.concatenate([v_refs[b][:, head_cols(h)], ones], axis=1),
                     preferred_element_type=jnp.float32) for b in range(n_blk)])
        o, l = ov[:, :HEAD_DIM], ov[:, HEAD_DIM:HEAD_DIM + 1]
        o_ref[:, head_cols(h)] = (o * (1.0 / l)).astype(o_ref.dtype)

    s_next = scores(0)
    for h in range(n_heads):
        s = s_next
        if h + 1 < n_heads:
            s_next = scores(h + 1)
        weighted_values(h, softmax(s))


def _bias_rows(rel_table, tq, width):
    n_rel = rel_table.shape[-1]
    far = LEFT - REL_MAX
    t = rel_table.astype(jnp.float32)
    oldest = jnp.broadcast_to(t[:, n_rel - 1:], (t.shape[0], far))
    newest = jnp.broadcast_to(t[:, :1], (t.shape[0], width - tq - far - n_rel))
    wrapped = jnp.broadcast_to(t[:, n_rel - 1:], (t.shape[0], tq))
    u = jnp.concatenate([oldest, t[:, ::-1], newest, wrapped], axis=1)
    return u.reshape(t.shape[0], 1, width)


def _attn(q, k, v, rel_table):
    s, d = q.shape
    n_heads = d // HEAD_DIM
    tq = ATTN_TQ
    n_blk = LEFT // tq + 1
    width = (n_blk + 1) * tq
    u = _bias_rows(rel_table, tq, width)
    kern = functools.partial(_attn_kernel, n_blk=n_blk, tq=tq, n_heads=n_heads)
    q_spec = pl.BlockSpec((tq, d), lambda i: (i, 0))

    def kv_spec(b):
        return pl.BlockSpec((tq, d), lambda i: (jnp.maximum(i - (n_blk - 1) + b, 0), 0))

    kv_specs = [kv_spec(b) for b in range(n_blk)]
    bias_shape = (n_heads, n_blk + 1, tq, tq)
    score_bytes = n_heads * n_blk * _nbytes((tq, tq), jnp.float32)
    vmem = ((4 + 4 * n_blk) * _nbytes((tq, d), MXU_DTYPE) + _nbytes(bias_shape, jnp.float32)
            + _nbytes(u.shape, jnp.float32) + score_bytes)
    return pl.pallas_call(
        kern,
        grid=(s // tq,),
        in_specs=[q_spec] + kv_specs + kv_specs + [_resident(u.shape, lambda i: (0, 0, 0))],
        out_specs=q_spec,
        out_shape=jax.ShapeDtypeStruct((s, d), MXU_DTYPE),
        scratch_shapes=[pltpu.VMEM(bias_shape, jnp.float32)],
        compiler_params=_params(("arbitrary",), vmem),
        name="attn",
    )(q, *([k] * n_blk), *([v] * n_blk), u)


def _oproj_kernel(a_ref, w_ref, x_ref, o_ref, *, tn):
    a = a_ref[...]
    for c in range(0, w_ref.shape[1], tn):
        o_ref[:, c:c + tn] = x_ref[:, c:c + tn] + jnp.dot(
            a, w_ref[:, c:c + tn], preferred_element_type=jnp.float32)


def _oproj(a, w, w_layer, x):
    s, d = x.shape
    tm = PROJ_TM
    kd = a.shape[1]
    vmem = (2 * _nbytes((tm, kd), MXU_DTYPE) + _nbytes((kd, d), MXU_DTYPE) + 4 * _nbytes((tm, d), jnp.float32))
    return pl.pallas_call(
        functools.partial(_oproj_kernel, tn=PROJ_TN),
        grid=(s // tm,),
        in_specs=[
            pl.BlockSpec((tm, kd), lambda i: (i, 0)),
            _resident((None, kd, d), lambda i: (w_layer, 0, 0)),
            pl.BlockSpec((tm, d), lambda i: (i, 0)),
        ],
        out_specs=pl.BlockSpec((tm, d), lambda i: (i, 0)),
        out_shape=jax.ShapeDtypeStruct((s, d), jnp.float32),
        compiler_params=_params(("parallel",), vmem),
        name="oproj",
    )(a, w, x)


def kernel(x, ffn1_norm, ffn1_w_gate, ffn1_w_up, ffn1_w_down, mix_norm, ffn2_norm, ffn2_w_gate, ffn2_w_up, ffn2_w_down, pool_w, pool_scale, kv_norm, w_k, w_v, k_gain, w_q, q_gain, rel_bias, w_o):
    b, s, d = x.shape
    depth = ffn1_norm.shape[0]
    n_a = pool_w.shape[0]
    cast = lambda w: w.astype(MXU_DTYPE)
    row = lambda g: g.reshape(g.shape[0], 1, g.shape[1])
    f1 = (row(ffn1_norm), ffn1_w_gate, ffn1_w_up, ffn1_w_down)
    f2 = (row(ffn2_norm), ffn2_w_gate, ffn2_w_up, ffn2_w_down)
    mix_g = row(mix_norm)
    pool_wc, pool_sc = cast(pool_w), row(pool_scale)
    wk, wv, wq, wo = cast(w_k)[None], cast(w_v)[None], cast(w_q), cast(w_o)
    kv_g = kv_norm.reshape(1, 1, d)

    outs = []
    for bi in range(b):
        xs = x.reshape(b * s, d) if b == 1 else x[bi]
        k = v = None
        for l in range(depth):
            xs = _ffn(xs, *f1, l)
            if l < n_a:
                xs = _pool(xs, mix_g, pool_wc, pool_sc, l)
            else:
                a = l - n_a
                (q,) = _proj(xs, mix_g, l, q_gain[a].reshape(1, HEAD_DIM), (wq,), a, (True,),
                             out_scale=HEAD_DIM ** -0.5)
                o = _attn(q, k, v, rel_bias[a])
                xs = _oproj(o, wo, a, xs)
            xs = _ffn(xs, *f2, l)
            if l == n_a - 1:
                k, v = _proj(xs, kv_g, 0, k_gain.reshape(1, HEAD_DIM), (wk, wv), 0, (True, False))
        outs.append(xs)
    return outs[0].reshape(b, s, d) if b == 1 else jnp.stack(outs, axis=0)
```

```python
import functools

import jax
import jax.numpy as jnp
from jax import lax
from jax.experimental import pallas as pl
from jax.experimental.pallas import tpu as pltpu

CHUNK = 64
POOL_WINDOWS = (2, 4, 8, 16)
HEAD_DIM = 128
LEFT_CHUNKS = 8
LEFT = LEFT_CHUNKS * CHUNK
REL_MAX = 128
EPS = 1e-6
NEG_INF = -1e30

MXU_DTYPE = jnp.bfloat16
POOL_HALO = 16

V7X_VMEM_BYTES = 64 * 1024 * 1024
VMEM_RESERVE_BYTES = 2 * 1024 * 1024
VMEM_SLACK_BYTES = 8 * 1024 * 1024

FFN_TM = 1024
FFN_TF = 512
FFN_CAST_TF = 512
FFN_ROW_CHUNK = 512
PROJ_TM = 512
PROJ_TN = 512
ATTN_TQ = 256
POOL_TM = 512


def _rms(x, g):
    ms = jnp.mean(x * x, axis=-1, keepdims=True)
    return (x * lax.rsqrt(ms + EPS)) * g


def _nbytes(shape, dtype):
    n = jnp.dtype(dtype).itemsize
    for s in shape:
        n *= s
    return n


def _params(sem, window_bytes):
    limit = min(window_bytes + VMEM_SLACK_BYTES, V7X_VMEM_BYTES - VMEM_RESERVE_BYTES)
    return pltpu.CompilerParams(dimension_semantics=sem, vmem_limit_bytes=limit)


def _resident(block_shape, index_map):
    return pl.BlockSpec(block_shape, index_map, pipeline_mode=pl.Buffered(1))


def _ffn_kernel(*refs, n_f, tf, rem, cast_out):
    if cast_out:
        x_ref, g_ref, wg_ref, wu_ref, wd_ref, o_ref, wgu_ref, h_ref = refs
    else:
        x_ref, g_ref, wgu_ref, wd_ref, o_ref, h_ref = refs
    j = pl.program_id(1)
    row_chunks = [slice(r, r + FFN_ROW_CHUNK) for r in range(0, x_ref.shape[0], FFN_ROW_CHUNK)]

    def weights(width):
        wd = wd_ref[:width, :].astype(MXU_DTYPE)
        if cast_out:
            wgu_ref[0, :, :width] = wg_ref[:, :width].astype(MXU_DTYPE)
            wgu_ref[1, :, :width] = wu_ref[:, :width].astype(MXU_DTYPE)
            if width < tf:
                wgu_ref[:, :, width:] = jnp.zeros((2, wgu_ref.shape[1], tf - width), MXU_DTYPE)
        return wgu_ref[0, :, :width], wgu_ref[1, :, :width], wd

    def swiglu(h, wg, wu, wd):
        gate = jnp.dot(h, wg, preferred_element_type=jnp.float32)
        up = jnp.dot(h, wu, preferred_element_type=jnp.float32)
        a = (jax.nn.silu(gate) * up).astype(h.dtype)
        return jnp.dot(a, wd, preferred_element_type=jnp.float32)

    @pl.when(j == 0)
    def _():
        w = weights(tf)
        g = g_ref[...]
        for rows in row_chunks:
            h = _rms(x_ref[rows, :], g).astype(h_ref.dtype)
            h_ref[rows, :] = h
            o_ref[rows, :] = swiglu(h, *w)

    @pl.when((j > 0) & (j < n_f - 1))
    def _():
        o_ref[...] += swiglu(h_ref[...], *weights(tf))

    @pl.when(j == n_f - 1)
    def _():
        o_ref[...] = x_ref[...] + 0.5 * (o_ref[...] + swiglu(h_ref[...], *weights(rem)))


def _ffn_call(x, gain, layer, gate_up, wd, *, f, tile0, n_tiles, tf, cast_out):
    s, d = x.shape
    tm = FFN_TM
    n_f = pl.cdiv(f, tf)
    assert n_f >= 2, "the kernel's first and last hidden-column steps must be distinct"
    rem = f - (n_f - 1) * tf
    kern = functools.partial(_ffn_kernel, n_f=n_f, tf=tf, rem=rem, cast_out=cast_out)
    x_map = lambda i, j: (i + tile0, 0)
    x_spec = _resident((tm, d), x_map) if n_tiles == 1 else pl.BlockSpec((tm, d), x_map)
    if cast_out:
        gu_specs = [pl.BlockSpec((None, d, tf), lambda i, j: (layer, 0, j))] * 2
    else:
        assert tf == FFN_TF
        gu_specs = [pl.BlockSpec((None, 2, d, tf), lambda i, j: (j, 0, 0, 0))]
    wd_spec = pl.BlockSpec((None, tf, d), lambda i, j: (layer, j, 0))
    in_specs = [x_spec, pl.BlockSpec((None, 1, d), lambda i, j: (layer, 0, 0))] + gu_specs + [wd_spec]
    out_specs = [pl.BlockSpec((tm, d), x_map)]
    out_shape = [jax.ShapeDtypeStruct((s, d), jnp.float32)]
    x_bufs = 1 if n_tiles == 1 else 2
    vmem = ((x_bufs + 2) * _nbytes((tm, d), jnp.float32) + 4 * _nbytes((d, tf), gate_up[0].dtype)
            + 2 * _nbytes((tf, d), wd.dtype) + _nbytes((tf, d), MXU_DTYPE) + _nbytes((tm, d), MXU_DTYPE))
    if cast_out:
        per_tile = FFN_TF // tf
        n_tile = pl.cdiv(f, FFN_TF)
        assert per_tile * tf == FFN_TF and n_f == n_tile * per_tile, "column blocks must tile the padded copies"
        out_specs.append(pl.BlockSpec((None, 2, d, tf), lambda i, j: (j // per_tile, 0, 0, j % per_tile)))
        out_shape.append(jax.ShapeDtypeStruct((n_tile, 2, d, FFN_TF), MXU_DTYPE))
        vmem += 4 * _nbytes((d, tf), MXU_DTYPE)
    args = [x, gain, *gate_up, wd]
    return pl.pallas_call(
        kern,
        grid=(n_tiles, n_f),
        in_specs=in_specs,
        out_specs=out_specs,
        out_shape=out_shape,
        scratch_shapes=[pltpu.VMEM((tm, d), MXU_DTYPE)],
        input_output_aliases={0: 0},
        compiler_params=_params(("arbitrary", "arbitrary"), vmem),
        name="ffn_cast" if cast_out else "ffn",
    )(*args)


def _ffn(x, gain, wg, wu, wd, layer):
    n_tiles = x.shape[0] // FFN_TM
    f = wg.shape[-1]
    out, wgu = _ffn_call(x, gain, layer, (wg, wu), wd, f=f, tile0=0, n_tiles=1, tf=FFN_CAST_TF, cast_out=True)
    if n_tiles == 1:
        return out
    (out,) = _ffn_call(out, gain, layer, (wgu,), wd, f=f, tile0=1, n_tiles=n_tiles - 1, tf=FFN_TF, cast_out=False)
    return out


def _pool_kernel(x_ref, halo_ref, g_ref, w_ref, sc_ref, o_ref, *, tm, group):
    i = pl.program_id(0)
    x = x_ref[...]
    g = g_ref[...]
    h = _rms(x, g)
    hh = _rms(halo_ref[...], g)
    hh = jnp.where(i == 0, 0.0, hh)
    ext = jnp.concatenate([hh, h], axis=0)
    t = i * tm + lax.broadcasted_iota(jnp.int32, (tm, 1), 0)
    for gi, w in enumerate(POOL_WINDOWS):
        cols = slice(gi * group, (gi + 1) * group)
        acc = ext[:, cols]
        k = 1
        while k < w:
            acc = acc + pltpu.roll(acc, k, axis=0)
            k *= 2
        cnt = jnp.minimum(t + 1, w).astype(jnp.float32)
        pooled = acc[POOL_HALO:, :] / cnt
        diff = (pooled - h[:, cols]).astype(w_ref.dtype)
        y = jnp.dot(diff, w_ref[gi], preferred_element_type=jnp.float32)
        o_ref[:, cols] = x[:, cols] + y * sc_ref[:, cols]


def _pool(x, gain, w_pool, scale, layer):
    s, d = x.shape
    n_g, group = w_pool.shape[1], w_pool.shape[2]
    tm = POOL_TM
    hb = tm // POOL_HALO
    kern = functools.partial(_pool_kernel, tm=tm, group=group)
    vmem = (4 * _nbytes((tm, d), jnp.float32) + 2 * _nbytes((POOL_HALO, d), jnp.float32)
            + 2 * _nbytes(w_pool.shape[1:], MXU_DTYPE) + 6 * _nbytes((tm + POOL_HALO, d), jnp.float32))
    return pl.pallas_call(
        kern,
        grid=(s // tm,),
        in_specs=[
            pl.BlockSpec((tm, d), lambda i: (i, 0)),
            pl.BlockSpec((POOL_HALO, d), lambda i: (jnp.maximum(i * hb - 1, 0), 0)),
            pl.BlockSpec((None, 1, d), lambda i: (layer, 0, 0)),
            pl.BlockSpec((None, n_g, group, group), lambda i: (layer, 0, 0, 0)),
            pl.BlockSpec((None, 1, d), lambda i: (layer, 0, 0)),
        ],
        out_specs=pl.BlockSpec((tm, d), lambda i: (i, 0)),
        out_shape=jax.ShapeDtypeStruct((s, d), jnp.float32),
        compiler_params=_params(("parallel",), vmem),
        name="pool",
    )(x, x, gain, w_pool, scale)


def _proj_kernel(*refs, n_w, head_norm, out_scale, tn):
    x_ref, g_ref, hg_ref = refs[0], refs[1], refs[2]
    w_refs = refs[3:3 + n_w]
    o_refs = refs[3 + n_w:3 + 2 * n_w]
    h = _rms(x_ref[...], g_ref[...]).astype(w_refs[0].dtype)
    hg = hg_ref[...] * out_scale
    for w_ref, o_ref, hn in zip(w_refs, o_refs, head_norm):
        for c in range(0, w_ref.shape[1], tn):
            acc = jnp.dot(h, w_ref[:, c:c + tn], preferred_element_type=jnp.float32)
            if hn:
                acc = jnp.concatenate(
                    [_rms(acc[:, e:e + HEAD_DIM], hg) for e in range(0, tn, HEAD_DIM)], axis=1)
            o_ref[:, c:c + tn] = acc.astype(o_ref.dtype)


def _proj(x, gain, layer, head_gain, weights, w_layer, head_norm, out_scale=1.0):
    s, d = x.shape
    n = weights[0].shape[-1]
    tm = PROJ_TM
    n_w = len(weights)
    kern = functools.partial(_proj_kernel, n_w=n_w, head_norm=tuple(head_norm), out_scale=out_scale, tn=PROJ_TN)
    w_spec = _resident((None, d, n), lambda i: (w_layer, 0, 0))
    o_spec = pl.BlockSpec((tm, n), lambda i: (i, 0))
    vmem = (2 * _nbytes((tm, d), jnp.float32) + n_w * _nbytes((d, n), MXU_DTYPE)
            + 2 * n_w * _nbytes((tm, n), MXU_DTYPE) + _nbytes((tm, d), MXU_DTYPE))
    return pl.pallas_call(
        kern,
        grid=(s // tm,),
        in_specs=[
            pl.BlockSpec((tm, d), lambda i: (i, 0)),
            pl.BlockSpec((None, 1, d), lambda i: (layer, 0, 0)),
            pl.BlockSpec((1, HEAD_DIM), lambda i: (0, 0)),
        ] + [w_spec] * n_w,
        out_specs=[o_spec] * n_w,
        out_shape=[jax.ShapeDtypeStruct((s, n), MXU_DTYPE)] * n_w,
        compiler_params=_params(("parallel",), vmem),
        name="proj",
    )(x, gain, head_gain, *weights)


def _attn_kernel(*refs, n_blk, tq, n_heads):
    q_ref = refs[0]
    k_refs = refs[1:1 + n_blk]
    v_refs = refs[1 + n_blk:1 + 2 * n_blk]
    u_ref = refs[1 + 2 * n_blk]
    o_ref = refs[2 + 2 * n_blk]
    bias_ref = refs[3 + 2 * n_blk]
    i = pl.program_id(0)
    cpb = tq // CHUNK

    @pl.when(i == 0)
    def _():
        rq = lax.broadcasted_iota(jnp.int32, (tq, tq), 0) // CHUNK + (n_blk - 1) * cpb
        cc = lax.broadcasted_iota(jnp.int32, (tq, tq), 1) // CHUNK
        for h in range(n_heads):
            ub = jnp.broadcast_to(u_ref[h], (tq, u_ref.shape[-1]))
            t = pltpu.roll(ub, 0, axis=1, stride=1, stride_axis=0)
            for b in range(n_blk):
                ck = cc + b * cpb
                band = (ck <= rq) & (ck >= rq - LEFT_CHUNKS)
                bias_ref[h, b] = jnp.where(band, t[:, b * tq:(b + 1) * tq], NEG_INF)
            bias_ref[h, n_blk] = jnp.full((tq, tq), NEG_INF, jnp.float32)

    bias_idx = [jnp.where(i - (n_blk - 1) + b >= 0, b, n_blk) for b in range(n_blk - 1)] + [n_blk - 1]
    nt = (((1,), (1,)), ((), ()))

    def head_cols(h):
        return slice(h * HEAD_DIM, (h + 1) * HEAD_DIM)

    def scores(h):
        q = q_ref[:, head_cols(h)]
        return [lax.dot_general(q, k_refs[b][:, head_cols(h)], nt, preferred_element_type=jnp.float32)
                + bias_ref[h, bias_idx[b]] for b in range(n_blk)]

    def softmax(s):
        m = functools.reduce(jnp.maximum, s)
        m = jnp.max(m, axis=-1, keepdims=True)
        return [jnp.exp(sb - m).astype(MXU_DTYPE) for sb in s]

    ones = jnp.ones((tq, HEAD_DIM), MXU_DTYPE)

    def weighted_values(h, p):
        ov = functools.reduce(
            lambda a, c: a + c,
            [jnp.dot(p[b], jnp.concatenate([v_refs[b][:, head_cols(h)], ones], axis=1),
                     preferred_element_type=jnp.float32) for b in range(n_blk)])
        o, l = ov[:, :HEAD_DIM], ov[:, HEAD_DIM:HEAD_DIM + 1]
        o_ref[:, head_cols(h)] = (o * (1.0 / l)).astype(o_ref.dtype)

    s_next = scores(0)
    for h in range(n_heads):
        s = s_next
        if h + 1 < n_heads:
            s_next = scores(h + 1)
        weighted_values(h, softmax(s))


def _bias_rows(rel_table, tq, width):
    n_rel = rel_table.shape[-1]
    far = LEFT - REL_MAX
    t = rel_table.astype(jnp.float32)
    oldest = jnp.broadcast_to(t[:, n_rel - 1:], (t.shape[0], far))
    newest = jnp.broadcast_to(t[:, :1], (t.shape[0], width - tq - far - n_rel))
    wrapped = jnp.broadcast_to(t[:, n_rel - 1:], (t.shape[0], tq))
    u = jnp.concatenate([oldest, t[:, ::-1], newest, wrapped], axis=1)
    return u.reshape(t.shape[0], 1, width)


def _attn(q, k, v, rel_table):
    s, d = q.shape
    n_heads = d // HEAD_DIM
    tq = ATTN_TQ
    n_blk = LEFT // tq + 1
    width = (n_blk + 1) * tq
    u = _bias_rows(rel_table, tq, width)
    kern = functools.partial(_attn_kernel, n_blk=n_blk, tq=tq, n_heads=n_heads)
    q_spec = pl.BlockSpec((tq, d), lambda i: (i, 0))

    def kv_spec(b):
        return pl.BlockSpec((tq, d), lambda i: (jnp.maximum(i - (n_blk - 1) + b, 0), 0))

    kv_specs = [kv_spec(b) for b in range(n_blk)]
    bias_shape = (n_heads, n_blk + 1, tq, tq)
    score_bytes = n_heads * n_blk * _nbytes((tq, tq), jnp.float32)
    vmem = ((4 + 4 * n_blk) * _nbytes((tq, d), MXU_DTYPE) + _nbytes(bias_shape, jnp.float32)
            + _nbytes(u.shape, jnp.float32) + score_bytes)
    return pl.pallas_call(
        kern,
        grid=(s // tq,),
        in_specs=[q_spec] + kv_specs + kv_specs + [_resident(u.shape, lambda i: (0, 0, 0))],
        out_specs=q_spec,
        out_shape=jax.ShapeDtypeStruct((s, d), MXU_DTYPE),
        scratch_shapes=[pltpu.VMEM(bias_shape, jnp.float32)],
        compiler_params=_params(("arbitrary",), vmem),
        name="attn",
    )(q, *([k] * n_blk), *([v] * n_blk), u)


def _oproj_kernel(a_ref, w_ref, x_ref, o_ref, *, tn):
    a = a_ref[...]
    for c in range(0, w_ref.shape[1], tn):
        o_ref[:, c:c + tn] = x_ref[:, c:c + tn] + jnp.dot(
            a, w_ref[:, c:c + tn], preferred_element_type=jnp.float32)


def _oproj(a, w, w_layer, x):
    s, d = x.shape
    tm = PROJ_TM
    kd = a.shape[1]
    vmem = (2 * _nbytes((tm, kd), MXU_DTYPE) + _nbytes((kd, d), MXU_DTYPE) + 4 * _nbytes((tm, d), jnp.float32))
    return pl.pallas_call(
        functools.partial(_oproj_kernel, tn=PROJ_TN),
        grid=(s // tm,),
        in_specs=[
            pl.BlockSpec((tm, kd), lambda i: (i, 0)),
            _resident((None, kd, d), lambda i: (w_layer, 0, 0)),
            pl.BlockSpec((tm, d), lambda i: (i, 0)),
        ],
        out_specs=pl.BlockSpec((tm, d), lambda i: (i, 0)),
        out_shape=jax.ShapeDtypeStruct((s, d), jnp.float32),
        compiler_params=_params(("parallel",), vmem),
        name="oproj",
    )(a, w, x)


def kernel(x, ffn1_norm, ffn1_w_gate, ffn1_w_up, ffn1_w_down, mix_norm, ffn2_norm, ffn2_w_gate, ffn2_w_up, ffn2_w_down, pool_w, pool_scale, kv_norm, w_k, w_v, k_gain, w_q, q_gain, rel_bias, w_o):
    b, s, d = x.shape
    depth = ffn1_norm.shape[0]
    n_a = pool_w.shape[0]
    cast = lambda w: w.astype(MXU_DTYPE)
    row = lambda g: g.reshape(g.shape[0], 1, g.shape[1])
    f1 = (row(ffn1_norm), ffn1_w_gate, ffn1_w_up, ffn1_w_down)
    f2 = (row(ffn2_norm), ffn2_w_gate, ffn2_w_up, ffn2_w_down)
    mix_g = row(mix_norm)
    pool_wc, pool_sc = cast(pool_w), row(pool_scale)
    wk, wv, wq, wo = cast(w_k)[None], cast(w_v)[None], cast(w_q), cast(w_o)
    kv_g = kv_norm.reshape(1, 1, d)

    outs = []
    for bi in range(b):
        xs = x.reshape(b * s, d) if b == 1 else x[bi]
        k = v = None
        for l in range(depth):
            xs = _ffn(xs, *f1, l)
            if l < n_a:
                xs = _pool(xs, mix_g, pool_wc, pool_sc, l)
            else:
                a = l - n_a
                (q,) = _proj(xs, mix_g, l, q_gain[a].reshape(1, HEAD_DIM), (wq,), a, (True,),
                             out_scale=HEAD_DIM ** -0.5)
                o = _attn(q, k, v, rel_bias[a])
                xs = _oproj(o, wo, a, xs)
            xs = _ffn(xs, *f2, l)
            if l == n_a - 1:
                k, v = _proj(xs, kv_g, 0, k_gain.reshape(1, HEAD_DIM), (wk, wv), 0, (True, False))
        outs.append(xs)
    return outs[0].reshape(b, s, d) if b == 1 else jnp.stack(outs, axis=0)
```

```python
import functools

import jax
import jax.numpy as jnp
from jax import lax
from jax.experimental import pallas as pl
from jax.experimental.pallas import tpu as pltpu

CHUNK = 64
POOL_WINDOWS = (2, 4, 8, 16)
HEAD_DIM = 128
LEFT_CHUNKS = 8
LEFT = LEFT_CHUNKS * CHUNK
REL_MAX = 128
EPS = 1e-6
NEG_INF = -1e30

MXU_DTYPE = jnp.bfloat16
POOL_HALO = 16

V7X_VMEM_BYTES = 64 * 1024 * 1024
VMEM_RESERVE_BYTES = 2 * 1024 * 1024
VMEM_SLACK_BYTES = 8 * 1024 * 1024

FFN_TM = 1024
FFN_TF = 512
FFN_CAST_TF = 512
FFN_ROW_CHUNK = 512
PROJ_TM = 512
PROJ_TN = 512
ATTN_TQ = 256
POOL_TM = 512


def _rms(x, g):
    ms = jnp.mean(x * x, axis=-1, keepdims=True)
    return (x * lax.rsqrt(ms + EPS)) * g


def _nbytes(shape, dtype):
    n = jnp.dtype(dtype).itemsize
    for s in shape:
        n *= s
    return n


def _params(sem, window_bytes):
    limit = min(window_bytes + VMEM_SLACK_BYTES, V7X_VMEM_BYTES - VMEM_RESERVE_BYTES)
    return pltpu.CompilerParams(dimension_semantics=sem, vmem_limit_bytes=limit)


def _resident(block_shape, index_map):
    return pl.BlockSpec(block_shape, index_map, pipeline_mode=pl.Buffered(1))


def _ffn_kernel(*refs, n_f, tf, rem, cast_out, has_dest):
    if has_dest:
        n_in = 6 if cast_out else 5
        refs = refs[:n_in - 1] + refs[n_in:]
    if cast_out:
        x_ref, g_ref, wg_ref, wu_ref, wd_ref, o_ref, wgu_ref, h_ref = refs
    else:
        x_ref, g_ref, wgu_ref, wd_ref, o_ref, h_ref = refs
    j = pl.program_id(1)
    row_chunks = [slice(r, r + FFN_ROW_CHUNK) for r in range(0, x_ref.shape[0], FFN_ROW_CHUNK)]

    def weights(width):
        wd = wd_ref[:width, :].astype(MXU_DTYPE)
        if cast_out:
            wgu_ref[0, :, :width] = wg_ref[:, :width].astype(MXU_DTYPE)
            wgu_ref[1, :, :width] = wu_ref[:, :width].astype(MXU_DTYPE)
            if width < tf:
                wgu_ref[:, :, width:] = jnp.zeros((2, wgu_ref.shape[1], tf - width), MXU_DTYPE)
        return wgu_ref[0, :, :width], wgu_ref[1, :, :width], wd

    def swiglu(h, wg, wu, wd):
        gate = jnp.dot(h, wg, preferred_element_type=jnp.float32)
        up = jnp.dot(h, wu, preferred_element_type=jnp.float32)
        a = (jax.nn.silu(gate) * up).astype(h.dtype)
        return jnp.dot(a, wd, preferred_element_type=jnp.float32)

    @pl.when(j == 0)
    def _():
        w = weights(tf)
        g = g_ref[...]
        for rows in row_chunks:
            h = _rms(x_ref[rows, :], g).astype(h_ref.dtype)
            h_ref[rows, :] = h
            o_ref[rows, :] = swiglu(h, *w)

    @pl.when((j > 0) & (j < n_f - 1))
    def _():
        o_ref[...] += swiglu(h_ref[...], *weights(tf))

    @pl.when(j == n_f - 1)
    def _():
        o_ref[...] = x_ref[...] + 0.5 * (o_ref[...] + swiglu(h_ref[...], *weights(rem)))


def _ffn_call(x, gain, layer, gate_up, wd, *, f, tile0, n_tiles, tf, cast_out, dest=None):
    s, d = x.shape
    tm = FFN_TM
    n_f = pl.cdiv(f, tf)
    assert n_f >= 2, "the kernel's first and last hidden-column steps must be distinct"
    rem = f - (n_f - 1) * tf
    kern = functools.partial(_ffn_kernel, n_f=n_f, tf=tf, rem=rem, cast_out=cast_out, has_dest=dest is not None)
    x_map = lambda i, j: (i + tile0, 0)
    x_spec = _resident((tm, d), x_map) if n_tiles == 1 else pl.BlockSpec((tm, d), x_map)
    if cast_out:
        gu_specs = [pl.BlockSpec((None, d, tf), lambda i, j: (layer, 0, j))] * 2
    else:
        assert tf == FFN_TF
        gu_specs = [pl.BlockSpec((None, 2, d, tf), lambda i, j: (j, 0, 0, 0))]
    wd_spec = pl.BlockSpec((None, tf, d), lambda i, j: (layer, j, 0))
    in_specs = [x_spec, pl.BlockSpec((None, 1, d), lambda i, j: (layer, 0, 0))] + gu_specs + [wd_spec]
    out_specs = [pl.BlockSpec((tm, d), x_map)]
    out_shape = [jax.ShapeDtypeStruct((s, d), jnp.float32)]
    x_bufs = 1 if n_tiles == 1 else 2
    vmem = ((x_bufs + 2) * _nbytes((tm, d), jnp.float32) + 4 * _nbytes((d, tf), gate_up[0].dtype)
            + 2 * _nbytes((tf, d), wd.dtype) + _nbytes((tf, d), MXU_DTYPE) + _nbytes((tm, d), MXU_DTYPE))
    if cast_out:
        per_tile = FFN_TF // tf
        n_tile = pl.cdiv(f, FFN_TF)
        assert per_tile * tf == FFN_TF and n_f == n_tile * per_tile, "column blocks must tile the padded copies"
        out_specs.append(pl.BlockSpec((None, 2, d, tf), lambda i, j: (j // per_tile, 0, 0, j % per_tile)))
        out_shape.append(jax.ShapeDtypeStruct((n_tile, 2, d, FFN_TF), MXU_DTYPE))
        vmem += 4 * _nbytes((d, tf), MXU_DTYPE)
    args = [x, gain, *gate_up, wd]
    aliased = 0
    if dest is not None:
        in_specs.append(pl.BlockSpec(memory_space=pl.ANY))
        args.append(dest)
        aliased = len(args) - 1
    return pl.pallas_call(
        kern,
        grid=(n_tiles, n_f),
        in_specs=in_specs,
        out_specs=out_specs,
        out_shape=out_shape,
        scratch_shapes=[pltpu.VMEM((tm, d), MXU_DTYPE)],
        input_output_aliases={aliased: 0},
        compiler_params=_params(("arbitrary", "arbitrary"), vmem),
        name="ffn_cast" if cast_out else "ffn",
    )(*args)


def _ffn(x, gain, wg, wu, wd, layer, keep_x=False):
    n_tiles = x.shape[0] // FFN_TM
    f = wg.shape[-1]
    dest = jnp.zeros_like(x) if keep_x else None
    out, wgu = _ffn_call(x, gain, layer, (wg, wu), wd, f=f, tile0=0, n_tiles=1, tf=FFN_CAST_TF, cast_out=True,
                         dest=dest)
    if n_tiles == 1:
        return out
    (out,) = _ffn_call(x if keep_x else out, gain, layer, (wgu,), wd, f=f, tile0=1, n_tiles=n_tiles - 1, tf=FFN_TF,
                       cast_out=False, dest=out if keep_x else None)
    return out


def _pool_kernel(x_ref, halo_ref, g_ref, w_ref, sc_ref, o_ref, *, tm, group):
    i = pl.program_id(0)
    x = x_ref[...]
    g = g_ref[...]
    h = _rms(x, g)
    hh = _rms(halo_ref[...], g)
    hh = jnp.where(i == 0, 0.0, hh)
    ext = jnp.concatenate([hh, h], axis=0)
    t = i * tm + lax.broadcasted_iota(jnp.int32, (tm, 1), 0)
    for gi, w in enumerate(POOL_WINDOWS):
        cols = slice(gi * group, (gi + 1) * group)
        acc = ext[:, cols]
        k = 1
        while k < w:
            acc = acc + pltpu.roll(acc, k, axis=0)
            k *= 2
        cnt = jnp.minimum(t + 1, w).astype(jnp.float32)
        pooled = acc[POOL_HALO:, :] / cnt
        diff = (pooled - h[:, cols]).astype(w_ref.dtype)
        y = jnp.dot(diff, w_ref[gi], preferred_element_type=jnp.float32)
        o_ref[:, cols] = x[:, cols] + y * sc_ref[:, cols]


def _pool(x, gain, w_pool, scale, layer):
    s, d = x.shape
    n_g, group = w_pool.shape[1], w_pool.shape[2]
    tm = POOL_TM
    hb = tm // POOL_HALO
    kern = functools.partial(_pool_kernel, tm=tm, group=group)
    vmem = (4 * _nbytes((tm, d), jnp.float32) + 2 * _nbytes((POOL_HALO, d), jnp.float32)
            + 2 * _nbytes(w_pool.shape[1:], MXU_DTYPE) + 6 * _nbytes((tm + POOL_HALO, d), jnp.float32))
    return pl.pallas_call(
        kern,
        grid=(s // tm,),
        in_specs=[
            pl.BlockSpec((tm, d), lambda i: (i, 0)),
            pl.BlockSpec((POOL_HALO, d), lambda i: (jnp.maximum(i * hb - 1, 0), 0)),
            pl.BlockSpec((None, 1, d), lambda i: (layer, 0, 0)),
            pl.BlockSpec((None, n_g, group, group), lambda i: (layer, 0, 0, 0)),
            pl.BlockSpec((None, 1, d), lambda i: (layer, 0, 0)),
        ],
        out_specs=pl.BlockSpec((tm, d), lambda i: (i, 0)),
        out_shape=jax.ShapeDtypeStruct((s, d), jnp.float32),
        compiler_params=_params(("parallel",), vmem),
        name="pool",
    )(x, x, gain, w_pool, scale)


def _proj_kernel(*refs, n_w, head_norm, out_scale, tn):
    x_ref, g_ref, hg_ref = refs[0], refs[1], refs[2]
    w_refs = refs[3:3 + n_w]
    o_refs = refs[3 + n_w:3 + 2 * n_w]
    h = _rms(x_ref[...], g_ref[...]).astype(w_refs[0].dtype)
    hg = hg_ref[...] * out_scale
    for w_ref, o_ref, hn in zip(w_refs, o_refs, head_norm):
        for c in range(0, w_ref.shape[1], tn):
            acc = jnp.dot(h, w_ref[:, c:c + tn], preferred_element_type=jnp.float32)
            if hn:
                acc = jnp.concatenate(
                    [_rms(acc[:, e:e + HEAD_DIM], hg) for e in range(0, tn, HEAD_DIM)], axis=1)
            o_ref[:, c:c + tn] = acc.astype(o_ref.dtype)


def _proj(x, gain, layer, head_gain, weights, w_layer, head_norm, out_scale=1.0):
    s, d = x.shape
    n = weights[0].shape[-1]
    tm = PROJ_TM
    n_w = len(weights)
    kern = functools.partial(_proj_kernel, n_w=n_w, head_norm=tuple(head_norm), out_scale=out_scale, tn=PROJ_TN)
    w_spec = _resident((None, d, n), lambda i: (w_layer, 0, 0))
    o_spec = pl.BlockSpec((tm, n), lambda i: (i, 0))
    vmem = (2 * _nbytes((tm, d), jnp.float32) + n_w * _nbytes((d, n), MXU_DTYPE)
            + 2 * n_w * _nbytes((tm, n), MXU_DTYPE) + _nbytes((tm, d), MXU_DTYPE))
    return pl.pallas_call(
        kern,
        grid=(s // tm,),
        in_specs=[
            pl.BlockSpec((tm, d), lambda i: (i, 0)),
            pl.BlockSpec((None, 1, d), lambda i: (layer, 0, 0)),
            pl.BlockSpec((1, HEAD_DIM), lambda i: (0, 0)),
        ] + [w_spec] * n_w,
        out_specs=[o_spec] * n_w,
        out_shape=[jax.ShapeDtypeStruct((s, n), MXU_DTYPE)] * n_w,
        compiler_params=_params(("parallel",), vmem),
        name="proj",
    )(x, gain, head_gain, *weights)


def _attn_kernel(*refs, n_blk, tq, n_heads):
    q_ref = refs[0]
    k_refs = refs[1:1 + n_blk]
    v_refs = refs[1 + n_blk:1 + 2 * n_blk]
    u_ref = refs[1 + 2 * n_blk]
    o_ref = refs[2 + 2 * n_blk]
    bias_ref = refs[3 + 2 * n_blk]
    i = pl.program_id(0)
    cpb = tq // CHUNK

    @pl.when(i == 0)
    def _():
        rq = lax.broadcasted_iota(jnp.int32, (tq, tq), 0) // CHUNK + (n_blk - 1) * cpb
        cc = lax.broadcasted_iota(jnp.int32, (tq, tq), 1) // CHUNK
        for h in range(n_heads):
            ub = jnp.broadcast_to(u_ref[h], (tq, u_ref.shape[-1]))
            t = pltpu.roll(ub, 0, axis=1, stride=1, stride_axis=0)
            for b in range(n_blk):
                ck = cc + b * cpb
                band = (ck <= rq) & (ck >= rq - LEFT_CHUNKS)
                bias_ref[h, b] = jnp.where(band, t[:, b * tq:(b + 1) * tq], NEG_INF)
            bias_ref[h, n_blk] = jnp.full((tq, tq), NEG_INF, jnp.float32)

    bias_idx = [jnp.where(i - (n_blk - 1) + b >= 0, b, n_blk) for b in range(n_blk - 1)] + [n_blk - 1]
    nt = (((1,), (1,)), ((), ()))

    def head_cols(h):
        return slice(h * HEAD_DIM, (h + 1) * HEAD_DIM)

    def scores(h):
        q = q_ref[:, head_cols(h)]
        return [lax.dot_general(q, k_refs[b][:, head_cols(h)], nt, preferred_element_type=jnp.float32)
                + bias_ref[h, bias_idx[b]] for b in range(n_blk)]

    def softmax(s):
        m = functools.reduce(jnp.maximum, s)
        m = jnp.max(m, axis=-1, keepdims=True)
        return [jnp.exp(sb - m).astype(MXU_DTYPE) for sb in s]

    ones = jnp.ones((tq, HEAD_DIM), MXU_DTYPE)

    def weighted_values(h, p):
        ov = functools.reduce(
            lambda a, c: a + c,
            [jnp.dot(p[b], jnp.concatenate([v_refs[b][:, head_cols(h)], ones], axis=1),
                     preferred_element_type=jnp.float32) for b in range(n_blk)])
        o, l = ov[:, :HEAD_DIM], ov[:, HEAD_DIM:HEAD_DIM + 1]
        o_ref[:, head_cols(h)] = (o * (1.0 / l)).astype(o_ref.dtype)

    s_next = scores(0)
    for h in range(n_heads):
        s = s_next
        if h + 1 < n_heads:
            s_next = scores(h + 1)
        weighted_values(h, softmax(s))


def _bias_rows(rel_table, tq, width):
    n_rel = rel_table.shape[-1]
    far = LEFT - REL_MAX
    t = rel_table.astype(jnp.float32)
    oldest = jnp.broadcast_to(t[:, n_rel - 1:], (t.shape[0], far))
    newest = jnp.broadcast_to(t[:, :1], (t.shape[0], width - tq - far - n_rel))
    wrapped = jnp.broadcast_to(t[:, n_rel - 1:], (t.shape[0], tq))
    u = jnp.concatenate([oldest, t[:, ::-1], newest, wrapped], axis=1)
    return u.reshape(t.shape[0], 1, width)


def _attn(q, k, v, rel_table):
    s, d = q.shape
    n_heads = d // HEAD_DIM
    tq = ATTN_TQ
    n_blk = LEFT // tq + 1
    width = (n_blk + 1) * tq
    u = _bias_rows(rel_table, tq, width)
    kern = functools.partial(_attn_kernel, n_blk=n_blk, tq=tq, n_heads=n_heads)
    q_spec = pl.BlockSpec((tq, d), lambda i: (i, 0))

    def kv_spec(b):
        return pl.BlockSpec((tq, d), lambda i: (jnp.maximum(i - (n_blk - 1) + b, 0), 0))

    kv_specs = [kv_spec(b) for b in range(n_blk)]
    bias_shape = (n_heads, n_blk + 1, tq, tq)
    score_bytes = n_heads * n_blk * _nbytes((tq, tq), jnp.float32)
    vmem = ((4 + 4 * n_blk) * _nbytes((tq, d), MXU_DTYPE) + _nbytes(bias_shape, jnp.float32)
            + _nbytes(u.shape, jnp.float32) + score_bytes)
    return pl.pallas_call(
        kern,
        grid=(s // tq,),
        in_specs=[q_spec] + kv_specs + kv_specs + [_resident(u.shape, lambda i: (0, 0, 0))],
        out_specs=q_spec,
        out_shape=jax.ShapeDtypeStruct((s, d), MXU_DTYPE),
        scratch_shapes=[pltpu.VMEM(bias_shape, jnp.float32)],
        compiler_params=_params(("arbitrary",), vmem),
        name="attn",
    )(q, *([k] * n_blk), *([v] * n_blk), u)


def _oproj_kernel(a_ref, w_ref, x_ref, o_ref, *, tn):
    a = a_ref[...]
    for c in range(0, w_ref.shape[1], tn):
        o_ref[:, c:c + tn] = x_ref[:, c:c + tn] + jnp.dot(
            a, w_ref[:, c:c + tn], preferred_element_type=jnp.float32)


def _oproj(a, w, w_layer, x):
    s, d = x.shape
    tm = PROJ_TM
    kd = a.shape[1]
    vmem = (2 * _nbytes((tm, kd), MXU_DTYPE) + _nbytes((kd, d), MXU_DTYPE) + 4 * _nbytes((tm, d), jnp.float32))
    return pl.pallas_call(
        functools.partial(_oproj_kernel, tn=PROJ_TN),
        grid=(s // tm,),
        in_specs=[
            pl.BlockSpec((tm, kd), lambda i: (i, 0)),
            _resident((None, kd, d), lambda i: (w_layer, 0, 0)),
            pl.BlockSpec((tm, d), lambda i: (i, 0)),
        ],
        out_specs=pl.BlockSpec((tm, d), lambda i: (i, 0)),
        out_shape=jax.ShapeDtypeStruct((s, d), jnp.float32),
        compiler_params=_params(("parallel",), vmem),
        name="oproj",
    )(a, w, x)


def kernel(x, ffn1_norm, ffn1_w_gate, ffn1_w_up, ffn1_w_down, mix_norm, ffn2_norm, ffn2_w_gate, ffn2_w_up, ffn2_w_down, pool_w, pool_scale, kv_norm, w_k, w_v, k_gain, w_q, q_gain, rel_bias, w_o):
    b, s, d = x.shape
    depth = ffn1_norm.shape[0]
    n_a = pool_w.shape[0]
    cast = lambda w: w.astype(MXU_DTYPE)
    row = lambda g: g.reshape(g.shape[0], 1, g.shape[1])
    f1 = (row(ffn1_norm), ffn1_w_gate, ffn1_w_up, ffn1_w_down)
    f2 = (row(ffn2_norm), ffn2_w_gate, ffn2_w_up, ffn2_w_down)
    mix_g = row(mix_norm)
    pool_wc, pool_sc = cast(pool_w), row(pool_scale)
    wk, wv, wq, wo = cast(w_k)[None], cast(w_v)[None], cast(w_q), cast(w_o)
    kv_g = kv_norm.reshape(1, 1, d)

    outs = []
    for bi in range(b):
        xs = x.reshape(b * s, d) if b == 1 else x[bi]
        k = v = None
        for l in range(depth):
            xs = _ffn(xs, *f1, l, keep_x=(l == 0))
            if l < n_a:
                xs = _pool(xs, mix_g, pool_wc, pool_sc, l)
            else:
                a = l - n_a
                (q,) = _proj(xs, mix_g, l, q_gain[a].reshape(1, HEAD_DIM), (wq,), a, (True,),
                             out_scale=HEAD_DIM ** -0.5)
                o = _attn(q, k, v, rel_bias[a])
                xs = _oproj(o, wo, a, xs)
            xs = _ffn(xs, *f2, l)
            if l == n_a - 1:
                k, v = _proj(xs, kv_g, 0, k_gain.reshape(1, HEAD_DIM), (wk, wv), 0, (True, False))
        outs.append(xs)
    return outs[0].reshape(b, s, d) if b == 1 else jnp.stack(outs, axis=0)
```

```python
import functools

import jax
import jax.numpy as jnp
from jax import lax
from jax.experimental import pallas as pl
from jax.experimental.pallas import tpu as pltpu

CHUNK = 64
POOL_WINDOWS = (2, 4, 8, 16)
HEAD_DIM = 128
LEFT_CHUNKS = 8
LEFT = LEFT_CHUNKS * CHUNK
REL_MAX = 128
EPS = 1e-6
NEG_INF = -1e30

MXU_DTYPE = jnp.bfloat16
POOL_HALO = 16

V7X_VMEM_BYTES = 64 * 1024 * 1024
VMEM_RESERVE_BYTES = 2 * 1024 * 1024
VMEM_SLACK_BYTES = 8 * 1024 * 1024

FFN_TM = 1024
FFN_TF = 512
FFN_CAST_TF = 512
FFN_ROW_CHUNK = 512
PROJ_TM = 512
PROJ_TN = 512
ATTN_TQ = 256
POOL_TM = 512


def _rms(x, g):
    ms = jnp.mean(x * x, axis=-1, keepdims=True)
    return (x * lax.rsqrt(ms + EPS)) * g


def _nbytes(shape, dtype):
    n = jnp.dtype(dtype).itemsize
    for s in shape:
        n *= s
    return n


def _params(sem, window_bytes):
    limit = min(window_bytes + VMEM_SLACK_BYTES, V7X_VMEM_BYTES - VMEM_RESERVE_BYTES)
    return pltpu.CompilerParams(dimension_semantics=sem, vmem_limit_bytes=limit)


def _resident(block_shape, index_map):
    return pl.BlockSpec(block_shape, index_map, pipeline_mode=pl.Buffered(1))


def _ffn_kernel(*refs, layer, n_f, tf, rem, cast_out, has_dest):
    if has_dest:
        n_in = 6 if cast_out else 5
        refs = refs[:n_in - 1] + refs[n_in:]
    if cast_out:
        x_ref, g_ref, wg_ref, wu_ref, wd_ref, o_ref, wgu_ref, h_ref = refs
    else:
        x_ref, g_ref, wgu_ref, wd_ref, o_ref, h_ref = refs
    j = pl.program_id(1)
    row_chunks = [slice(r, r + FFN_ROW_CHUNK) for r in range(0, x_ref.shape[0], FFN_ROW_CHUNK)]

    def weights(width):
        wd = wd_ref[:width, :].astype(MXU_DTYPE)
        if cast_out:
            wgu_ref[0, :, :width] = wg_ref[:, :width].astype(MXU_DTYPE)
            wgu_ref[1, :, :width] = wu_ref[:, :width].astype(MXU_DTYPE)
            if width < tf:
                wgu_ref[:, :, width:] = jnp.zeros((2, wgu_ref.shape[1], tf - width), MXU_DTYPE)
        return wgu_ref[0, :, :width], wgu_ref[1, :, :width], wd

    def swiglu(h, wg, wu, wd):
        gate = jnp.dot(h, wg, preferred_element_type=jnp.float32)
        up = jnp.dot(h, wu, preferred_element_type=jnp.float32)
        a = (jax.nn.silu(gate) * up).astype(h.dtype)
        return jnp.dot(a, wd, preferred_element_type=jnp.float32)

    @pl.when(j == 0)
    def _():
        w = weights(tf)
        g = g_ref[layer:layer + 1, :]
        for rows in row_chunks:
            h = _rms(x_ref[rows, :], g).astype(h_ref.dtype)
            h_ref[rows, :] = h
            o_ref[rows, :] = swiglu(h, *w)

    @pl.when((j > 0) & (j < n_f - 1))
    def _():
        o_ref[...] += swiglu(h_ref[...], *weights(tf))

    @pl.when(j == n_f - 1)
    def _():
        o_ref[...] = x_ref[...] + 0.5 * (o_ref[...] + swiglu(h_ref[...], *weights(rem)))


def _ffn_call(x, gain, layer, gate_up, wd, *, f, tile0, n_tiles, tf, cast_out, dest=None):
    s, d = x.shape
    tm = FFN_TM
    n_f = pl.cdiv(f, tf)
    assert n_f >= 2, "the kernel's first and last hidden-column steps must be distinct"
    rem = f - (n_f - 1) * tf
    kern = functools.partial(_ffn_kernel, layer=layer, n_f=n_f, tf=tf, rem=rem, cast_out=cast_out,
                             has_dest=dest is not None)
    x_map = lambda i, j: (i + tile0, 0)
    x_spec = _resident((tm, d), x_map) if n_tiles == 1 else pl.BlockSpec((tm, d), x_map)
    if cast_out:
        gu_specs = [pl.BlockSpec((None, d, tf), lambda i, j: (layer, 0, j))] * 2
    else:
        assert tf == FFN_TF
        gu_specs = [pl.BlockSpec((None, 2, d, tf), lambda i, j: (j, 0, 0, 0))]
    wd_spec = pl.BlockSpec((None, tf, d), lambda i, j: (layer, j, 0))
    in_specs = [x_spec, pl.BlockSpec(gain.shape, lambda i, j: (0, 0))] + gu_specs + [wd_spec]
    out_specs = [pl.BlockSpec((tm, d), x_map)]
    out_shape = [jax.ShapeDtypeStruct((s, d), jnp.float32)]
    x_bufs = 1 if n_tiles == 1 else 2
    vmem = ((x_bufs + 2) * _nbytes((tm, d), jnp.float32) + 4 * _nbytes((d, tf), gate_up[0].dtype)
            + 2 * _nbytes((tf, d), wd.dtype) + _nbytes((tf, d), MXU_DTYPE) + _nbytes((tm, d), MXU_DTYPE))
    if cast_out:
        per_tile = FFN_TF // tf
        n_tile = pl.cdiv(f, FFN_TF)
        assert per_tile * tf == FFN_TF and n_f == n_tile * per_tile, "column blocks must tile the padded copies"
        out_specs.append(pl.BlockSpec((None, 2, d, tf), lambda i, j: (j // per_tile, 0, 0, j % per_tile)))
        out_shape.append(jax.ShapeDtypeStruct((n_tile, 2, d, FFN_TF), MXU_DTYPE))
        vmem += 4 * _nbytes((d, tf), MXU_DTYPE)
    args = [x, gain, *gate_up, wd]
    aliased = 0
    if dest is not None:
        in_specs.append(pl.BlockSpec(memory_space=pl.ANY))
        args.append(dest)
        aliased = len(args) - 1
    return pl.pallas_call(
        kern,
        grid=(n_tiles, n_f),
        in_specs=in_specs,
        out_specs=out_specs,
        out_shape=out_shape,
        scratch_shapes=[pltpu.VMEM((tm, d), MXU_DTYPE)],
        input_output_aliases={aliased: 0},
        compiler_params=_params(("arbitrary", "arbitrary"), vmem),
        name="ffn_cast" if cast_out else "ffn",
    )(*args)


def _ffn(x, gain, wg, wu, wd, layer, keep_x=False):
    n_tiles = x.shape[0] // FFN_TM
    f = wg.shape[-1]
    dest = jnp.zeros_like(x) if keep_x else None
    out, wgu = _ffn_call(x, gain, layer, (wg, wu), wd, f=f, tile0=0, n_tiles=1, tf=FFN_CAST_TF, cast_out=True,
                         dest=dest)
    if n_tiles == 1:
        return out
    (out,) = _ffn_call(x if keep_x else out, gain, layer, (wgu,), wd, f=f, tile0=1, n_tiles=n_tiles - 1, tf=FFN_TF,
                       cast_out=False, dest=out if keep_x else None)
    return out


def _pool_kernel(x_ref, halo_ref, g_ref, w_ref, sc_ref, o_ref, *, layer, tm, group):
    i = pl.program_id(0)
    x = x_ref[...]
    g = g_ref[layer:layer + 1, :]
    sc = sc_ref[layer:layer + 1, :]
    h = _rms(x, g)
    hh = _rms(halo_ref[...], g)
    hh = jnp.where(i == 0, 0.0, hh)
    ext = jnp.concatenate([hh, h], axis=0)
    t = i * tm + lax.broadcasted_iota(jnp.int32, (tm, 1), 0)
    for gi, w in enumerate(POOL_WINDOWS):
        cols = slice(gi * group, (gi + 1) * group)
        acc = ext[:, cols]
        k = 1
        while k < w:
            acc = acc + pltpu.roll(acc, k, axis=0)
            k *= 2
        cnt = jnp.minimum(t + 1, w).astype(jnp.float32)
        pooled = acc[POOL_HALO:, :] / cnt
        diff = (pooled - h[:, cols]).astype(w_ref.dtype)
        y = jnp.dot(diff, w_ref[gi], preferred_element_type=jnp.float32)
        o_ref[:, cols] = x[:, cols] + y * sc[:, cols]


def _pool(x, gain, w_pool, scale, layer):
    s, d = x.shape
    n_g, group = w_pool.shape[1], w_pool.shape[2]
    tm = POOL_TM
    hb = tm // POOL_HALO
    kern = functools.partial(_pool_kernel, layer=layer, tm=tm, group=group)
    vmem = (4 * _nbytes((tm, d), jnp.float32) + 2 * _nbytes((POOL_HALO, d), jnp.float32)
            + 2 * _nbytes(w_pool.shape[1:], MXU_DTYPE) + 6 * _nbytes((tm + POOL_HALO, d), jnp.float32))
    return pl.pallas_call(
        kern,
        grid=(s // tm,),
        in_specs=[
            pl.BlockSpec((tm, d), lambda i: (i, 0)),
            pl.BlockSpec((POOL_HALO, d), lambda i: (jnp.maximum(i * hb - 1, 0), 0)),
            pl.BlockSpec(gain.shape, lambda i: (0, 0)),
            pl.BlockSpec((None, n_g, group, group), lambda i: (layer, 0, 0, 0)),
            pl.BlockSpec(scale.shape, lambda i: (0, 0)),
        ],
        out_specs=pl.BlockSpec((tm, d), lambda i: (i, 0)),
        out_shape=jax.ShapeDtypeStruct((s, d), jnp.float32),
        compiler_params=_params(("parallel",), vmem),
        name="pool",
    )(x, x, gain, w_pool, scale)


def _cast_weights_once(w_refs, wc_refs, tn):
    @pl.when(pl.program_id(0) == 0)
    def _():
        for w_ref, wc_ref in zip(w_refs, wc_refs):
            for c in range(0, w_ref.shape[1], tn):
                wc_ref[:, c:c + tn] = w_ref[:, c:c + tn].astype(wc_ref.dtype)


def _proj_kernel(*refs, layer, n_w, head_norm, out_scale, tn, cast_w):
    x_ref, g_ref, hg_ref = refs[0], refs[1], refs[2]
    w_refs = refs[3:3 + n_w]
    o_refs = refs[3 + n_w:3 + 2 * n_w]
    if cast_w:
        _cast_weights_once(w_refs, refs[3 + 2 * n_w:], tn)
        w_refs = refs[3 + 2 * n_w:]
    h = _rms(x_ref[...], g_ref[layer:layer + 1, :]).astype(w_refs[0].dtype)
    hg = hg_ref[...] * out_scale
    for w_ref, o_ref, hn in zip(w_refs, o_refs, head_norm):
        for c in range(0, w_ref.shape[1], tn):
            acc = jnp.dot(h, w_ref[:, c:c + tn], preferred_element_type=jnp.float32)
            if hn:
                acc = jnp.concatenate(
                    [_rms(acc[:, e:e + HEAD_DIM], hg) for e in range(0, tn, HEAD_DIM)], axis=1)
            o_ref[:, c:c + tn] = acc.astype(o_ref.dtype)


def _proj(x, gain, layer, head_gain, weights, w_layer, head_norm, out_scale=1.0):
    s, d = x.shape
    n = weights[0].shape[-1]
    tm = PROJ_TM
    n_w = len(weights)
    cast_w = weights[0].dtype != MXU_DTYPE
    kern = functools.partial(_proj_kernel, layer=layer, n_w=n_w, head_norm=tuple(head_norm), out_scale=out_scale,
                             tn=PROJ_TN, cast_w=cast_w)
    w_spec = _resident((None, d, n), lambda i: (w_layer, 0, 0))
    o_spec = pl.BlockSpec((tm, n), lambda i: (i, 0))
    scratch = [pltpu.VMEM((d, n), MXU_DTYPE)] * n_w if cast_w else []
    vmem = (2 * _nbytes((tm, d), jnp.float32) + n_w * _nbytes((d, n), weights[0].dtype)
            + len(scratch) * _nbytes((d, n), MXU_DTYPE)
            + 2 * n_w * _nbytes((tm, n), MXU_DTYPE) + _nbytes((tm, d), MXU_DTYPE))
    return pl.pallas_call(
        kern,
        grid=(s // tm,),
        in_specs=[
            pl.BlockSpec((tm, d), lambda i: (i, 0)),
            pl.BlockSpec(gain.shape, lambda i: (0, 0)),
            pl.BlockSpec((1, HEAD_DIM), lambda i: (0, 0)),
        ] + [w_spec] * n_w,
        out_specs=[o_spec] * n_w,
        out_shape=[jax.ShapeDtypeStruct((s, n), MXU_DTYPE)] * n_w,
        scratch_shapes=scratch,
        compiler_params=_params(("arbitrary" if cast_w else "parallel",), vmem),
        name="proj",
    )(x, gain, head_gain, *weights)


def _attn_kernel(*refs, n_blk, tq, n_heads):
    q_ref = refs[0]
    k_refs = refs[1:1 + n_blk]
    v_refs = refs[1 + n_blk:1 + 2 * n_blk]
    u_ref = refs[1 + 2 * n_blk]
    o_ref = refs[2 + 2 * n_blk]
    bias_ref = refs[3 + 2 * n_blk]
    i = pl.program_id(0)
    cpb = tq // CHUNK

    @pl.when(i == 0)
    def _():
        rq = lax.broadcasted_iota(jnp.int32, (tq, tq), 0) // CHUNK + (n_blk - 1) * cpb
        cc = lax.broadcasted_iota(jnp.int32, (tq, tq), 1) // CHUNK
        for h in range(n_heads):
            ub = jnp.broadcast_to(u_ref[h], (tq, u_ref.shape[-1]))
            t = pltpu.roll(ub, 0, axis=1, stride=1, stride_axis=0)
            for b in range(n_blk):
                ck = cc + b * cpb
                band = (ck <= rq) & (ck >= rq - LEFT_CHUNKS)
                bias_ref[h, b] = jnp.where(band, t[:, b * tq:(b + 1) * tq], NEG_INF)
            bias_ref[h, n_blk] = jnp.full((tq, tq), NEG_INF, jnp.float32)

    bias_idx = [jnp.where(i - (n_blk - 1) + b >= 0, b, n_blk) for b in range(n_blk - 1)] + [n_blk - 1]
    nt = (((1,), (1,)), ((), ()))

    def head_cols(h):
        return slice(h * HEAD_DIM, (h + 1) * HEAD_DIM)

    def scores(h):
        q = q_ref[:, head_cols(h)]
        return [lax.dot_general(q, k_refs[b][:, head_cols(h)], nt, preferred_element_type=jnp.float32)
                + bias_ref[h, bias_idx[b]] for b in range(n_blk)]

    def softmax(s):
        m = functools.reduce(jnp.maximum, s)
        m = jnp.max(m, axis=-1, keepdims=True)
        return [jnp.exp(sb - m).astype(MXU_DTYPE) for sb in s]

    ones = jnp.ones((tq, HEAD_DIM), MXU_DTYPE)

    def weighted_values(h, p):
        ov = functools.reduce(
            lambda a, c: a + c,
            [jnp.dot(p[b], jnp.concatenate([v_refs[b][:, head_cols(h)], ones], axis=1),
                     preferred_element_type=jnp.float32) for b in range(n_blk)])
        o, l = ov[:, :HEAD_DIM], ov[:, HEAD_DIM:HEAD_DIM + 1]
        o_ref[:, head_cols(h)] = (o * (1.0 / l)).astype(o_ref.dtype)

    s_next = scores(0)
    for h in range(n_heads):
        s = s_next
        if h + 1 < n_heads:
            s_next = scores(h + 1)
        weighted_values(h, softmax(s))


def _bias_rows(rel_table, tq, width):
    n_rel = rel_table.shape[-1]
    far = LEFT - REL_MAX
    t = rel_table.astype(jnp.float32)
    oldest = jnp.broadcast_to(t[:, n_rel - 1:], (t.shape[0], far))
    newest = jnp.broadcast_to(t[:, :1], (t.shape[0], width - tq - far - n_rel))
    wrapped = jnp.broadcast_to(t[:, n_rel - 1:], (t.shape[0], tq))
    u = jnp.concatenate([oldest, t[:, ::-1], newest, wrapped], axis=1)
    return u.reshape(t.shape[0], 1, width)


def _attn(q, k, v, rel_table):
    s, d = q.shape
    n_heads = d // HEAD_DIM
    tq = ATTN_TQ
    n_blk = LEFT // tq + 1
    width = (n_blk + 1) * tq
    u = _bias_rows(rel_table, tq, width)
    kern = functools.partial(_attn_kernel, n_blk=n_blk, tq=tq, n_heads=n_heads)
    q_spec = pl.BlockSpec((tq, d), lambda i: (i, 0))

    def kv_spec(b):
        return pl.BlockSpec((tq, d), lambda i: (jnp.maximum(i - (n_blk - 1) + b, 0), 0))

    kv_specs = [kv_spec(b) for b in range(n_blk)]
    bias_shape = (n_heads, n_blk + 1, tq, tq)
    score_bytes = n_heads * n_blk * _nbytes((tq, tq), jnp.float32)
    vmem = ((4 + 4 * n_blk) * _nbytes((tq, d), MXU_DTYPE) + _nbytes(bias_shape, jnp.float32)
            + _nbytes(u.shape, jnp.float32) + score_bytes)
    return pl.pallas_call(
        kern,
        grid=(s // tq,),
        in_specs=[q_spec] + kv_specs + kv_specs + [_resident(u.shape, lambda i: (0, 0, 0))],
        out_specs=q_spec,
        out_shape=jax.ShapeDtypeStruct((s, d), MXU_DTYPE),
        scratch_shapes=[pltpu.VMEM(bias_shape, jnp.float32)],
        compiler_params=_params(("arbitrary",), vmem),
        name="attn",
    )(q, *([k] * n_blk), *([v] * n_blk), u)


def _oproj_kernel(a_ref, w_ref, x_ref, o_ref, wc_ref, *, tn):
    _cast_weights_once([w_ref], [wc_ref], tn)
    a = a_ref[...]
    for c in range(0, wc_ref.shape[1], tn):
        o_ref[:, c:c + tn] = x_ref[:, c:c + tn] + jnp.dot(
            a, wc_ref[:, c:c + tn], preferred_element_type=jnp.float32)


def _oproj(a, w, w_layer, x):
    s, d = x.shape
    tm = PROJ_TM
    kd = a.shape[1]
    vmem = (2 * _nbytes((tm, kd), MXU_DTYPE) + _nbytes((kd, d), w.dtype) + _nbytes((kd, d), MXU_DTYPE)
            + 4 * _nbytes((tm, d), jnp.float32))
    return pl.pallas_call(
        functools.partial(_oproj_kernel, tn=PROJ_TN),
        grid=(s // tm,),
        in_specs=[
            pl.BlockSpec((tm, kd), lambda i: (i, 0)),
            _resident((None, kd, d), lambda i: (w_layer, 0, 0)),
            pl.BlockSpec((tm, d), lambda i: (i, 0)),
        ],
        out_specs=pl.BlockSpec((tm, d), lambda i: (i, 0)),
        out_shape=jax.ShapeDtypeStruct((s, d), jnp.float32),
        scratch_shapes=[pltpu.VMEM((kd, d), MXU_DTYPE)],
        compiler_params=_params(("arbitrary",), vmem),
        name="oproj",
    )(a, w, x)


def kernel(x, ffn1_norm, ffn1_w_gate, ffn1_w_up, ffn1_w_down, mix_norm, ffn2_norm, ffn2_w_gate, ffn2_w_up, ffn2_w_down, pool_w, pool_scale, kv_norm, w_k, w_v, k_gain, w_q, q_gain, rel_bias, w_o):
    b, s, d = x.shape
    depth = ffn1_norm.shape[0]
    n_a = pool_w.shape[0]
    cast = lambda w: w.astype(MXU_DTYPE)
    f1 = (ffn1_norm, ffn1_w_gate, ffn1_w_up, ffn1_w_down)
    f2 = (ffn2_norm, ffn2_w_gate, ffn2_w_up, ffn2_w_down)
    pool_wc = cast(pool_w)
    wk, wv = cast(w_k)[None], cast(w_v)[None]
    kv_g = kv_norm.reshape(1, d)

    outs = []
    for bi in range(b):
        xs = x.reshape(b * s, d) if b == 1 else x[bi]
        k = v = None
        for l in range(depth):
            xs = _ffn(xs, *f1, l, keep_x=(l == 0))
            if l < n_a:
                xs = _pool(xs, mix_norm, pool_wc, pool_scale, l)
            else:
                a = l - n_a
                (q,) = _proj(xs, mix_norm, l, q_gain[a].reshape(1, HEAD_DIM), (w_q,), a, (True,),
                             out_scale=HEAD_DIM ** -0.5)
                o = _attn(q, k, v, rel_bias[a])
                xs = _oproj(o, w_o, a, xs)
            xs = _ffn(xs, *f2, l)
            if l == n_a - 1:
                k, v = _proj(xs, kv_g, 0, k_gain.reshape(1, HEAD_DIM), (wk, wv), 0, (True, False))
        outs.append(xs)
    return outs[0].reshape(b, s, d) if b == 1 else jnp.stack(outs, axis=0)
```

```python
import functools

import jax
import jax.numpy as jnp
from jax import lax
from jax.experimental import pallas as pl
from jax.experimental.pallas import tpu as pltpu

CHUNK = 64
POOL_WINDOWS = (2, 4, 8, 16)
HEAD_DIM = 128
LEFT_CHUNKS = 8
LEFT = LEFT_CHUNKS * CHUNK
REL_MAX = 128
EPS = 1e-6
NEG_INF = -1e30

MXU_DTYPE = jnp.bfloat16
POOL_HALO = 16

V7X_VMEM_BYTES = 64 * 1024 * 1024
VMEM_RESERVE_BYTES = 2 * 1024 * 1024
VMEM_SLACK_BYTES = 8 * 1024 * 1024

FFN_TM = 1024
FFN_TF = 512
FFN_CAST_TF = 512
FFN_ROW_CHUNK = 512
PROJ_TM = 512
PROJ_TN = 512
PROJ_ROW_CHUNK = 256
ATTN_TQ = 256
POOL_TM = 512


def _rms(x, g):
    ms = jnp.mean(x * x, axis=-1, keepdims=True)
    return (x * lax.rsqrt(ms + EPS)) * g


def _nbytes(shape, dtype):
    n = jnp.dtype(dtype).itemsize
    for s in shape:
        n *= s
    return n


def _params(sem, window_bytes):
    limit = min(window_bytes + VMEM_SLACK_BYTES, V7X_VMEM_BYTES - VMEM_RESERVE_BYTES)
    return pltpu.CompilerParams(dimension_semantics=sem, vmem_limit_bytes=limit)


def _resident(block_shape, index_map):
    return pl.BlockSpec(block_shape, index_map, pipeline_mode=pl.Buffered(1))


def _ffn_kernel(*refs, layer, n_f, tf, rem, cast_out, has_dest):
    if has_dest:
        n_in = 6 if cast_out else 5
        refs = refs[:n_in - 1] + refs[n_in:]
    if cast_out:
        x_ref, g_ref, wg_ref, wu_ref, wd_ref, o_ref, wgu_ref, h_ref = refs
    else:
        x_ref, g_ref, wgu_ref, wd_ref, o_ref, h_ref = refs
    j = pl.program_id(1)
    row_chunks = [slice(r, r + FFN_ROW_CHUNK) for r in range(0, x_ref.shape[0], FFN_ROW_CHUNK)]

    def weights(width):
        wd = wd_ref[:width, :].astype(MXU_DTYPE)
        if cast_out:
            wgu_ref[0, :, :width] = wg_ref[:, :width].astype(MXU_DTYPE)
            wgu_ref[1, :, :width] = wu_ref[:, :width].astype(MXU_DTYPE)
            if width < tf:
                wgu_ref[:, :, width:] = jnp.zeros((2, wgu_ref.shape[1], tf - width), MXU_DTYPE)
        return wgu_ref[0, :, :width], wgu_ref[1, :, :width], wd

    def swiglu(h, wg, wu, wd):
        gate = jnp.dot(h, wg, preferred_element_type=jnp.float32)
        up = jnp.dot(h, wu, preferred_element_type=jnp.float32)
        a = (jax.nn.silu(gate) * up).astype(h.dtype)
        return jnp.dot(a, wd, preferred_element_type=jnp.float32)

    @pl.when(j == 0)
    def _():
        w = weights(tf)
        g = g_ref[layer:layer + 1, :]
        for rows in row_chunks:
            h = _rms(x_ref[rows, :], g).astype(h_ref.dtype)
            h_ref[rows, :] = h
            o_ref[rows, :] = swiglu(h, *w)

    @pl.when((j > 0) & (j < n_f - 1))
    def _():
        o_ref[...] += swiglu(h_ref[...], *weights(tf))

    @pl.when(j == n_f - 1)
    def _():
        o_ref[...] = x_ref[...] + 0.5 * (o_ref[...] + swiglu(h_ref[...], *weights(rem)))


def _ffn_call(x, gain, layer, gate_up, wd, *, f, tile0, n_tiles, tf, cast_out, dest=None):
    s, d = x.shape
    tm = FFN_TM
    n_f = pl.cdiv(f, tf)
    assert n_f >= 2, "the kernel's first and last hidden-column steps must be distinct"
    rem = f - (n_f - 1) * tf
    kern = functools.partial(_ffn_kernel, layer=layer, n_f=n_f, tf=tf, rem=rem, cast_out=cast_out,
                             has_dest=dest is not None)
    x_map = lambda i, j: (i + tile0, 0)
    x_spec = _resident((tm, d), x_map) if n_tiles == 1 else pl.BlockSpec((tm, d), x_map)
    if cast_out:
        gu_specs = [pl.BlockSpec((None, d, tf), lambda i, j: (layer, 0, j))] * 2
    else:
        assert tf == FFN_TF
        gu_specs = [pl.BlockSpec((None, 2, d, tf), lambda i, j: (j, 0, 0, 0))]
    wd_spec = pl.BlockSpec((None, tf, d), lambda i, j: (layer, j, 0))
    in_specs = [x_spec, pl.BlockSpec(gain.shape, lambda i, j: (0, 0))] + gu_specs + [wd_spec]
    out_specs = [pl.BlockSpec((tm, d), x_map)]
    out_shape = [jax.ShapeDtypeStruct((s, d), jnp.float32)]
    x_bufs = 1 if n_tiles == 1 else 2
    vmem = ((x_bufs + 2) * _nbytes((tm, d), jnp.float32) + 4 * _nbytes((d, tf), gate_up[0].dtype)
            + 2 * _nbytes((tf, d), wd.dtype) + _nbytes((tf, d), MXU_DTYPE) + _nbytes((tm, d), MXU_DTYPE))
    if cast_out:
        per_tile = FFN_TF // tf
        n_tile = pl.cdiv(f, FFN_TF)
        assert per_tile * tf == FFN_TF and n_f == n_tile * per_tile, "column blocks must tile the padded copies"
        out_specs.append(pl.BlockSpec((None, 2, d, tf), lambda i, j: (j // per_tile, 0, 0, j % per_tile)))
        out_shape.append(jax.ShapeDtypeStruct((n_tile, 2, d, FFN_TF), MXU_DTYPE))
        vmem += 4 * _nbytes((d, tf), MXU_DTYPE)
    args = [x, gain, *gate_up, wd]
    aliased = 0
    if dest is not None:
        in_specs.append(pl.BlockSpec(memory_space=pl.ANY))
        args.append(dest)
        aliased = len(args) - 1
    return pl.pallas_call(
        kern,
        grid=(n_tiles, n_f),
        in_specs=in_specs,
        out_specs=out_specs,
        out_shape=out_shape,
        scratch_shapes=[pltpu.VMEM((tm, d), MXU_DTYPE)],
        input_output_aliases={aliased: 0},
        compiler_params=_params(("arbitrary", "arbitrary"), vmem),
        name="ffn_cast" if cast_out else "ffn",
    )(*args)


def _ffn(x, gain, wg, wu, wd, layer, keep_x=False):
    n_tiles = x.shape[0] // FFN_TM
    f = wg.shape[-1]
    dest = jnp.zeros_like(x) if keep_x else None
    out, wgu = _ffn_call(x, gain, layer, (wg, wu), wd, f=f, tile0=0, n_tiles=1, tf=FFN_CAST_TF, cast_out=True,
                         dest=dest)
    if n_tiles == 1:
        return out
    (out,) = _ffn_call(x if keep_x else out, gain, layer, (wgu,), wd, f=f, tile0=1, n_tiles=n_tiles - 1, tf=FFN_TF,
                       cast_out=False, dest=out if keep_x else None)
    return out


def _pool_kernel(x_ref, halo_ref, g_ref, w_ref, sc_ref, o_ref, *, layer, tm, group):
    i = pl.program_id(0)
    x = x_ref[...]
    g = g_ref[layer:layer + 1, :]
    sc = sc_ref[layer:layer + 1, :]
    h = _rms(x, g)
    hh = _rms(halo_ref[...], g)
    hh = jnp.where(i == 0, 0.0, hh)
    ext = jnp.concatenate([hh, h], axis=0)
    t = i * tm + lax.broadcasted_iota(jnp.int32, (tm, 1), 0)
    for gi, w in enumerate(POOL_WINDOWS):
        cols = slice(gi * group, (gi + 1) * group)
        acc = ext[:, cols]
        k = 1
        while k < w:
            acc = acc + pltpu.roll(acc, k, axis=0)
            k *= 2
        cnt = jnp.minimum(t + 1, w).astype(jnp.float32)
        pooled = acc[POOL_HALO:, :] / cnt
        diff = (pooled - h[:, cols]).astype(w_ref.dtype)
        y = jnp.dot(diff, w_ref[gi], preferred_element_type=jnp.float32)
        o_ref[:, cols] = x[:, cols] + y * sc[:, cols]


def _pool(x, gain, w_pool, scale, layer):
    s, d = x.shape
    n_g, group = w_pool.shape[1], w_pool.shape[2]
    tm = POOL_TM
    hb = tm // POOL_HALO
    kern = functools.partial(_pool_kernel, layer=layer, tm=tm, group=group)
    vmem = (4 * _nbytes((tm, d), jnp.float32) + 2 * _nbytes((POOL_HALO, d), jnp.float32)
            + 2 * _nbytes(w_pool.shape[1:], MXU_DTYPE) + 6 * _nbytes((tm + POOL_HALO, d), jnp.float32))
    return pl.pallas_call(
        kern,
        grid=(s // tm,),
        in_specs=[
            pl.BlockSpec((tm, d), lambda i: (i, 0)),
            pl.BlockSpec((POOL_HALO, d), lambda i: (jnp.maximum(i * hb - 1, 0), 0)),
            pl.BlockSpec(gain.shape, lambda i: (0, 0)),
            pl.BlockSpec((None, n_g, group, group), lambda i: (layer, 0, 0, 0)),
            pl.BlockSpec(scale.shape, lambda i: (0, 0)),
        ],
        out_specs=pl.BlockSpec((tm, d), lambda i: (i, 0)),
        out_shape=jax.ShapeDtypeStruct((s, d), jnp.float32),
        compiler_params=_params(("parallel",), vmem),
        name="pool",
    )(x, x, gain, w_pool, scale)


def _cast_weights_once(w_refs, wc_refs, tn):
    @pl.when(pl.program_id(0) == 0)
    def _():
        for w_ref, wc_ref in zip(w_refs, wc_refs):
            for c in range(0, w_ref.shape[1], tn):
                wc_ref[:, c:c + tn] = w_ref[:, c:c + tn].astype(wc_ref.dtype)


def _proj_kernel(*refs, layer, n_w, head_norm, out_scale, tn, cast_w):
    x_ref, g_ref, hg_ref = refs[0], refs[1], refs[2]
    w_refs = refs[3:3 + n_w]
    o_refs = refs[3 + n_w:3 + 2 * n_w]
    if cast_w:
        _cast_weights_once(w_refs, refs[3 + 2 * n_w:], tn)
        w_refs = refs[3 + 2 * n_w:]
    g = g_ref[layer:layer + 1, :]
    hg = hg_ref[...] * out_scale
    for r in range(0, x_ref.shape[0], PROJ_ROW_CHUNK):
        rows = slice(r, r + PROJ_ROW_CHUNK)
        h = _rms(x_ref[rows, :], g).astype(w_refs[0].dtype)
        for w_ref, o_ref, hn in zip(w_refs, o_refs, head_norm):
            for c in range(0, w_ref.shape[1], tn):
                acc = jnp.dot(h, w_ref[:, c:c + tn], preferred_element_type=jnp.float32)
                if hn:
                    acc = jnp.concatenate(
                        [_rms(acc[:, e:e + HEAD_DIM], hg) for e in range(0, tn, HEAD_DIM)], axis=1)
                o_ref[rows, c:c + tn] = acc.astype(o_ref.dtype)


def _proj(x, gain, layer, head_gain, weights, w_layer, head_norm, out_scale=1.0):
    s, d = x.shape
    n = weights[0].shape[-1]
    tm = PROJ_TM
    n_w = len(weights)
    cast_w = weights[0].dtype != MXU_DTYPE
    kern = functools.partial(_proj_kernel, layer=layer, n_w=n_w, head_norm=tuple(head_norm), out_scale=out_scale,
                             tn=PROJ_TN, cast_w=cast_w)
    w_spec = _resident((None, d, n), lambda i: (w_layer, 0, 0))
    o_spec = pl.BlockSpec((tm, n), lambda i: (i, 0))
    scratch = [pltpu.VMEM((d, n), MXU_DTYPE)] * n_w if cast_w else []
    vmem = (2 * _nbytes((tm, d), jnp.float32) + n_w * _nbytes((d, n), weights[0].dtype)
            + len(scratch) * _nbytes((d, n), MXU_DTYPE)
            + 2 * n_w * _nbytes((tm, n), MXU_DTYPE) + _nbytes((tm, d), MXU_DTYPE))
    return pl.pallas_call(
        kern,
        grid=(s // tm,),
        in_specs=[
            pl.BlockSpec((tm, d), lambda i: (i, 0)),
            pl.BlockSpec(gain.shape, lambda i: (0, 0)),
            pl.BlockSpec((1, HEAD_DIM), lambda i: (0, 0)),
        ] + [w_spec] * n_w,
        out_specs=[o_spec] * n_w,
        out_shape=[jax.ShapeDtypeStruct((s, n), MXU_DTYPE)] * n_w,
        scratch_shapes=scratch,
        compiler_params=_params(("arbitrary" if cast_w else "parallel",), vmem),
        name="proj",
    )(x, gain, head_gain, *weights)


def _attn_kernel(*refs, n_blk, tq, n_heads):
    q_ref = refs[0]
    k_refs = refs[1:1 + n_blk]
    v_refs = refs[1 + n_blk:1 + 2 * n_blk]
    u_ref = refs[1 + 2 * n_blk]
    o_ref = refs[2 + 2 * n_blk]
    bias_ref = refs[3 + 2 * n_blk]
    i = pl.program_id(0)
    cpb = tq // CHUNK

    @pl.when(i == 0)
    def _():
        rq = lax.broadcasted_iota(jnp.int32, (tq, tq), 0) // CHUNK + (n_blk - 1) * cpb
        cc = lax.broadcasted_iota(jnp.int32, (tq, tq), 1) // CHUNK
        for h in range(n_heads):
            ub = jnp.broadcast_to(u_ref[h], (tq, u_ref.shape[-1]))
            t = pltpu.roll(ub, 0, axis=1, stride=1, stride_axis=0)
            for b in range(n_blk):
                ck = cc + b * cpb
                band = (ck <= rq) & (ck >= rq - LEFT_CHUNKS)
                bias_ref[h, b] = jnp.where(band, t[:, b * tq:(b + 1) * tq], NEG_INF)
            bias_ref[h, n_blk] = jnp.full((tq, tq), NEG_INF, jnp.float32)

    bias_idx = [jnp.where(i - (n_blk - 1) + b >= 0, b, n_blk) for b in range(n_blk - 1)] + [n_blk - 1]
    nt = (((1,), (1,)), ((), ()))

    def head_cols(h):
        return slice(h * HEAD_DIM, (h + 1) * HEAD_DIM)

    def scores(h):
        q = q_ref[:, head_cols(h)]
        return [lax.dot_general(q, k_refs[b][:, head_cols(h)], nt, preferred_element_type=jnp.float32)
                + bias_ref[h, bias_idx[b]] for b in range(n_blk)]

    def softmax(s):
        m = functools.reduce(jnp.maximum, s)
        m = jnp.max(m, axis=-1, keepdims=True)
        return [jnp.exp(sb - m).astype(MXU_DTYPE) for sb in s]

    ones = jnp.ones((tq, HEAD_DIM), MXU_DTYPE)

    def weighted_values(h, p):
        ov = functools.reduce(
            lambda a, c: a + c,
            [jnp.dot(p[b], jnp.concatenate([v_refs[b][:, head_cols(h)], ones], axis=1),
                     preferred_element_type=jnp.float32) for b in range(n_blk)])
        o, l = ov[:, :HEAD_DIM], ov[:, HEAD_DIM:HEAD_DIM + 1]
        o_ref[:, head_cols(h)] = (o * (1.0 / l)).astype(o_ref.dtype)

    s_next = scores(0)
    for h in range(n_heads):
        s = s_next
        if h + 1 < n_heads:
            s_next = scores(h + 1)
        weighted_values(h, softmax(s))


def _bias_rows(rel_table, tq, width):
    n_rel = rel_table.shape[-1]
    far = LEFT - REL_MAX
    t = rel_table.astype(jnp.float32)
    oldest = jnp.broadcast_to(t[:, n_rel - 1:], (t.shape[0], far))
    newest = jnp.broadcast_to(t[:, :1], (t.shape[0], width - tq - far - n_rel))
    wrapped = jnp.broadcast_to(t[:, n_rel - 1:], (t.shape[0], tq))
    u = jnp.concatenate([oldest, t[:, ::-1], newest, wrapped], axis=1)
    return u.reshape(t.shape[0], 1, width)


def _attn(q, k, v, rel_table):
    s, d = q.shape
    n_heads = d // HEAD_DIM
    tq = ATTN_TQ
    n_blk = LEFT // tq + 1
    width = (n_blk + 1) * tq
    u = _bias_rows(rel_table, tq, width)
    kern = functools.partial(_attn_kernel, n_blk=n_blk, tq=tq, n_heads=n_heads)
    q_spec = pl.BlockSpec((tq, d), lambda i: (i, 0))

    def kv_spec(b):
        return pl.BlockSpec((tq, d), lambda i: (jnp.maximum(i - (n_blk - 1) + b, 0), 0))

    kv_specs = [kv_spec(b) for b in range(n_blk)]
    bias_shape = (n_heads, n_blk + 1, tq, tq)
    score_bytes = n_heads * n_blk * _nbytes((tq, tq), jnp.float32)
    vmem = ((4 + 4 * n_blk) * _nbytes((tq, d), MXU_DTYPE) + _nbytes(bias_shape, jnp.float32)
            + _nbytes(u.shape, jnp.float32) + score_bytes)
    return pl.pallas_call(
        kern,
        grid=(s // tq,),
        in_specs=[q_spec] + kv_specs + kv_specs + [_resident(u.shape, lambda i: (0, 0, 0))],
        out_specs=q_spec,
        out_shape=jax.ShapeDtypeStruct((s, d), MXU_DTYPE),
        scratch_shapes=[pltpu.VMEM(bias_shape, jnp.float32)],
        compiler_params=_params(("arbitrary",), vmem),
        name="attn",
    )(q, *([k] * n_blk), *([v] * n_blk), u)


def _oproj_kernel(a_ref, w_ref, x_ref, o_ref, wc_ref, *, tn):
    _cast_weights_once([w_ref], [wc_ref], tn)
    a = a_ref[...]
    for c in range(0, wc_ref.shape[1], tn):
        o_ref[:, c:c + tn] = x_ref[:, c:c + tn] + jnp.dot(
            a, wc_ref[:, c:c + tn], preferred_element_type=jnp.float32)


def _oproj(a, w, w_layer, x):
    s, d = x.shape
    tm = PROJ_TM
    kd = a.shape[1]
    vmem = (2 * _nbytes((tm, kd), MXU_DTYPE) + _nbytes((kd, d), w.dtype) + _nbytes((kd, d), MXU_DTYPE)
            + 4 * _nbytes((tm, d), jnp.float32))
    return pl.pallas_call(
        functools.partial(_oproj_kernel, tn=PROJ_TN),
        grid=(s // tm,),
        in_specs=[
            pl.BlockSpec((tm, kd), lambda i: (i, 0)),
            _resident((None, kd, d), lambda i: (w_layer, 0, 0)),
            pl.BlockSpec((tm, d), lambda i: (i, 0)),
        ],
        out_specs=pl.BlockSpec((tm, d), lambda i: (i, 0)),
        out_shape=jax.ShapeDtypeStruct((s, d), jnp.float32),
        scratch_shapes=[pltpu.VMEM((kd, d), MXU_DTYPE)],
        compiler_params=_params(("arbitrary",), vmem),
        name="oproj",
    )(a, w, x)


def kernel(x, ffn1_norm, ffn1_w_gate, ffn1_w_up, ffn1_w_down, mix_norm, ffn2_norm, ffn2_w_gate, ffn2_w_up, ffn2_w_down, pool_w, pool_scale, kv_norm, w_k, w_v, k_gain, w_q, q_gain, rel_bias, w_o):
    b, s, d = x.shape
    depth = ffn1_norm.shape[0]
    n_a = pool_w.shape[0]
    cast = lambda w: w.astype(MXU_DTYPE)
    f1 = (ffn1_norm, ffn1_w_gate, ffn1_w_up, ffn1_w_down)
    f2 = (ffn2_norm, ffn2_w_gate, ffn2_w_up, ffn2_w_down)
    pool_wc = cast(pool_w)
    wk, wv = cast(w_k)[None], cast(w_v)[None]
    kv_g = kv_norm.reshape(1, d)

    outs = []
    for bi in range(b):
        xs = x.reshape(b * s, d) if b == 1 else x[bi]
        k = v = None
        for l in range(depth):
            xs = _ffn(xs, *f1, l, keep_x=(l == 0))
            if l < n_a:
                xs = _pool(xs, mix_norm, pool_wc, pool_scale, l)
            else:
                a = l - n_a
                (q,) = _proj(xs, mix_norm, l, q_gain[a].reshape(1, HEAD_DIM), (w_q,), a, (True,),
                             out_scale=HEAD_DIM ** -0.5)
                o = _attn(q, k, v, rel_bias[a])
                xs = _oproj(o, w_o, a, xs)
            xs = _ffn(xs, *f2, l)
            if l == n_a - 1:
                k, v = _proj(xs, kv_g, 0, k_gain.reshape(1, HEAD_DIM), (wk, wv), 0, (True, False))
        outs.append(xs)
    return outs[0].reshape(b, s, d) if b == 1 else jnp.stack(outs, axis=0)
```

```python
import functools

import jax
import jax.numpy as jnp
from jax import lax
from jax.experimental import pallas as pl
from jax.experimental.pallas import tpu as pltpu

CHUNK = 64
POOL_WINDOWS = (2, 4, 8, 16)
HEAD_DIM = 128
LEFT_CHUNKS = 8
LEFT = LEFT_CHUNKS * CHUNK
REL_MAX = 128
EPS = 1e-6
NEG_INF = -1e30

MXU_DTYPE = jnp.bfloat16
POOL_HALO = 16

V7X_VMEM_BYTES = 64 * 1024 * 1024
VMEM_RESERVE_BYTES = 2 * 1024 * 1024
VMEM_SLACK_BYTES = 8 * 1024 * 1024

FFN_TM = 1024
FFN_TF = 512
FFN_CAST_TF = 512
FFN_ROW_CHUNK = 512
PROJ_TM = 512
PROJ_TN = 512
ATTN_TQ = 256
POOL_TM = 512


def _rms(x, g):
    ms = jnp.mean(x * x, axis=-1, keepdims=True)
    return (x * lax.rsqrt(ms + EPS)) * g


def _nbytes(shape, dtype):
    n = jnp.dtype(dtype).itemsize
    for s in shape:
        n *= s
    return n


def _params(sem, window_bytes):
    limit = min(window_bytes + VMEM_SLACK_BYTES, V7X_VMEM_BYTES - VMEM_RESERVE_BYTES)
    return pltpu.CompilerParams(dimension_semantics=sem, vmem_limit_bytes=limit)


def _resident(block_shape, index_map):
    return pl.BlockSpec(block_shape, index_map, pipeline_mode=pl.Buffered(1))


def _ffn_kernel(*refs, layer, n_f, tf, rem, cast_out, has_dest):
    if has_dest:
        n_in = 6 if cast_out else 5
        refs = refs[:n_in - 1] + refs[n_in:]
    if cast_out:
        x_ref, g_ref, wg_ref, wu_ref, wd_ref, o_ref, wgu_ref, h_ref = refs
    else:
        x_ref, g_ref, wgu_ref, wd_ref, o_ref, h_ref = refs
    j = pl.program_id(1)
    row_chunks = [slice(r, r + FFN_ROW_CHUNK) for r in range(0, x_ref.shape[0], FFN_ROW_CHUNK)]

    def weights(width):
        wd = wd_ref[:width, :].astype(MXU_DTYPE)
        if cast_out:
            wgu_ref[0, :, :width] = wg_ref[:, :width].astype(MXU_DTYPE)
            wgu_ref[1, :, :width] = wu_ref[:, :width].astype(MXU_DTYPE)
            if width < tf:
                wgu_ref[:, :, width:] = jnp.zeros((2, wgu_ref.shape[1], tf - width), MXU_DTYPE)
        return wgu_ref[0, :, :width], wgu_ref[1, :, :width], wd

    def swiglu(h, wg, wu, wd):
        gate = jnp.dot(h, wg, preferred_element_type=jnp.float32)
        up = jnp.dot(h, wu, preferred_element_type=jnp.float32)
        a = (jax.nn.silu(gate) * up).astype(h.dtype)
        return jnp.dot(a, wd, preferred_element_type=jnp.float32)

    @pl.when(j == 0)
    def _():
        w = weights(tf)
        g = g_ref[layer:layer + 1, :]
        for rows in row_chunks:
            h = _rms(x_ref[rows, :], g).astype(h_ref.dtype)
            h_ref[rows, :] = h
            o_ref[rows, :] = swiglu(h, *w)

    @pl.when((j > 0) & (j < n_f - 1))
    def _():
        o_ref[...] += swiglu(h_ref[...], *weights(tf))

    @pl.when(j == n_f - 1)
    def _():
        o_ref[...] = x_ref[...] + 0.5 * (o_ref[...] + swiglu(h_ref[...], *weights(rem)))


def _ffn_call(x, gain, layer, gate_up, wd, *, f, tile0, n_tiles, tf, cast_out, dest=None):
    s, d = x.shape
    tm = FFN_TM
    n_f = pl.cdiv(f, tf)
    assert n_f >= 2, "the kernel's first and last hidden-column steps must be distinct"
    rem = f - (n_f - 1) * tf
    kern = functools.partial(_ffn_kernel, layer=layer, n_f=n_f, tf=tf, rem=rem, cast_out=cast_out,
                             has_dest=dest is not None)
    x_map = lambda i, j: (i + tile0, 0)
    x_spec = _resident((tm, d), x_map) if n_tiles == 1 else pl.BlockSpec((tm, d), x_map)
    if cast_out:
        gu_specs = [pl.BlockSpec((None, d, tf), lambda i, j: (layer, 0, j))] * 2
    else:
        assert tf == FFN_TF
        gu_specs = [pl.BlockSpec((None, 2, d, tf), lambda i, j: (j, 0, 0, 0))]
    wd_spec = pl.BlockSpec((None, tf, d), lambda i, j: (layer, j, 0))
    in_specs = [x_spec, pl.BlockSpec(gain.shape, lambda i, j: (0, 0))] + gu_specs + [wd_spec]
    out_specs = [pl.BlockSpec((tm, d), x_map)]
    out_shape = [jax.ShapeDtypeStruct((s, d), jnp.float32)]
    x_bufs = 1 if n_tiles == 1 else 2
    vmem = ((x_bufs + 2) * _nbytes((tm, d), jnp.float32) + 4 * _nbytes((d, tf), gate_up[0].dtype)
            + 2 * _nbytes((tf, d), wd.dtype) + _nbytes((tf, d), MXU_DTYPE) + _nbytes((tm, d), MXU_DTYPE))
    if cast_out:
        per_tile = FFN_TF // tf
        n_tile = pl.cdiv(f, FFN_TF)
        assert per_tile * tf == FFN_TF and n_f == n_tile * per_tile, "column blocks must tile the padded copies"
        out_specs.append(pl.BlockSpec((None, 2, d, tf), lambda i, j: (j // per_tile, 0, 0, j % per_tile)))
        out_shape.append(jax.ShapeDtypeStruct((n_tile, 2, d, FFN_TF), MXU_DTYPE))
        vmem += 4 * _nbytes((d, tf), MXU_DTYPE)
    args = [x, gain, *gate_up, wd]
    aliased = 0
    if dest is not None:
        in_specs.append(pl.BlockSpec(memory_space=pl.ANY))
        args.append(dest)
        aliased = len(args) - 1
    return pl.pallas_call(
        kern,
        grid=(n_tiles, n_f),
        in_specs=in_specs,
        out_specs=out_specs,
        out_shape=out_shape,
        scratch_shapes=[pltpu.VMEM((tm, d), MXU_DTYPE)],
        input_output_aliases={aliased: 0},
        compiler_params=_params(("arbitrary", "arbitrary"), vmem),
        name="ffn_cast" if cast_out else "ffn",
    )(*args)


def _ffn(x, gain, wg, wu, wd, layer, keep_x=False):
    n_tiles = x.shape[0] // FFN_TM
    f = wg.shape[-1]
    dest = jnp.zeros_like(x) if keep_x else None
    out, wgu = _ffn_call(x, gain, layer, (wg, wu), wd, f=f, tile0=0, n_tiles=1, tf=FFN_CAST_TF, cast_out=True,
                         dest=dest)
    if n_tiles == 1:
        return out
    (out,) = _ffn_call(x if keep_x else out, gain, layer, (wgu,), wd, f=f, tile0=1, n_tiles=n_tiles - 1, tf=FFN_TF,
                       cast_out=False, dest=out if keep_x else None)
    return out


def _pool_kernel(x_ref, halo_ref, g_ref, w_ref, sc_ref, o_ref, *, layer, tm, group):
    i = pl.program_id(0)
    x = x_ref[...]
    g = g_ref[layer:layer + 1, :]
    sc = sc_ref[layer:layer + 1, :]
    h = _rms(x, g)
    hh = _rms(halo_ref[...], g)
    hh = jnp.where(i == 0, 0.0, hh)
    ext = jnp.concatenate([hh, h], axis=0)
    t = i * tm + lax.broadcasted_iota(jnp.int32, (tm, 1), 0)
    for gi, w in enumerate(POOL_WINDOWS):
        cols = slice(gi * group, (gi + 1) * group)
        acc = ext[:, cols]
        k = 1
        while k < w:
            acc = acc + pltpu.roll(acc, k, axis=0)
            k *= 2
        cnt = jnp.minimum(t + 1, w).astype(jnp.float32)
        pooled = acc[POOL_HALO:, :] / cnt
        diff = (pooled - h[:, cols]).astype(w_ref.dtype)
        y = jnp.dot(diff, w_ref[gi], preferred_element_type=jnp.float32)
        o_ref[:, cols] = x[:, cols] + y * sc[:, cols]


def _pool(x, gain, w_pool, scale, layer):
    s, d = x.shape
    n_g, group = w_pool.shape[1], w_pool.shape[2]
    tm = POOL_TM
    hb = tm // POOL_HALO
    kern = functools.partial(_pool_kernel, layer=layer, tm=tm, group=group)
    vmem = (4 * _nbytes((tm, d), jnp.float32) + 2 * _nbytes((POOL_HALO, d), jnp.float32)
            + 2 * _nbytes(w_pool.shape[1:], MXU_DTYPE) + 6 * _nbytes((tm + POOL_HALO, d), jnp.float32))
    return pl.pallas_call(
        kern,
        grid=(s // tm,),
        in_specs=[
            pl.BlockSpec((tm, d), lambda i: (i, 0)),
            pl.BlockSpec((POOL_HALO, d), lambda i: (jnp.maximum(i * hb - 1, 0), 0)),
            pl.BlockSpec(gain.shape, lambda i: (0, 0)),
            pl.BlockSpec((None, n_g, group, group), lambda i: (layer, 0, 0, 0)),
            pl.BlockSpec(scale.shape, lambda i: (0, 0)),
        ],
        out_specs=pl.BlockSpec((tm, d), lambda i: (i, 0)),
        out_shape=jax.ShapeDtypeStruct((s, d), jnp.float32),
        compiler_params=_params(("parallel",), vmem),
        name="pool",
    )(x, x, gain, w_pool, scale)


def _cast_weights_once(w_refs, wc_refs, tn):
    @pl.when(pl.program_id(0) == 0)
    def _():
        for w_ref, wc_ref in zip(w_refs, wc_refs):
            for c in range(0, w_ref.shape[1], tn):
                wc_ref[:, c:c + tn] = w_ref[:, c:c + tn].astype(wc_ref.dtype)


def _proj_kernel(*refs, layer, n_w, head_norm, out_scale, tn, cast_w):
    x_ref, g_ref, hg_ref = refs[0], refs[1], refs[2]
    w_refs = refs[3:3 + n_w]
    o_refs = refs[3 + n_w:3 + 2 * n_w]
    if cast_w:
        _cast_weights_once(w_refs, refs[3 + 2 * n_w:], tn)
        w_refs = refs[3 + 2 * n_w:]
    h = _rms(x_ref[...], g_ref[layer:layer + 1, :]).astype(w_refs[0].dtype)
    hg = hg_ref[...] * out_scale
    for w_ref, o_ref, hn in zip(w_refs, o_refs, head_norm):
        for c in range(0, w_ref.shape[1], tn):
            acc = jnp.dot(h, w_ref[:, c:c + tn], preferred_element_type=jnp.float32)
            if hn:
                acc = jnp.concatenate(
                    [_rms(acc[:, e:e + HEAD_DIM], hg) for e in range(0, tn, HEAD_DIM)], axis=1)
            o_ref[:, c:c + tn] = acc.astype(o_ref.dtype)


def _proj(x, gain, layer, head_gain, weights, w_layer, head_norm, out_scale=1.0):
    s, d = x.shape
    n = weights[0].shape[-1]
    tm = PROJ_TM
    n_w = len(weights)
    cast_w = weights[0].dtype != MXU_DTYPE
    kern = functools.partial(_proj_kernel, layer=layer, n_w=n_w, head_norm=tuple(head_norm), out_scale=out_scale,
                             tn=PROJ_TN, cast_w=cast_w)
    w_spec = _resident((None, d, n), lambda i: (w_layer, 0, 0))
    o_spec = pl.BlockSpec((tm, n), lambda i: (i, 0))
    scratch = [pltpu.VMEM((d, n), MXU_DTYPE)] * n_w if cast_w else []
    vmem = (2 * _nbytes((tm, d), jnp.float32) + n_w * _nbytes((d, n), weights[0].dtype)
            + len(scratch) * _nbytes((d, n), MXU_DTYPE)
            + 2 * n_w * _nbytes((tm, n), MXU_DTYPE) + _nbytes((tm, d), MXU_DTYPE))
    return pl.pallas_call(
        kern,
        grid=(s // tm,),
        in_specs=[
            pl.BlockSpec((tm, d), lambda i: (i, 0)),
            pl.BlockSpec(gain.shape, lambda i: (0, 0)),
            pl.BlockSpec((1, HEAD_DIM), lambda i: (0, 0)),
        ] + [w_spec] * n_w,
        out_specs=[o_spec] * n_w,
        out_shape=[jax.ShapeDtypeStruct((s, n), MXU_DTYPE)] * n_w,
        scratch_shapes=scratch,
        compiler_params=_params(("arbitrary" if cast_w else "parallel",), vmem),
        name="proj",
    )(x, gain, head_gain, *weights)


def _attn_kernel(*refs, n_blk, tq, n_heads):
    q_ref = refs[0]
    k_refs = refs[1:1 + n_blk]
    v_refs = refs[1 + n_blk:1 + 2 * n_blk]
    u_ref = refs[1 + 2 * n_blk]
    o_ref = refs[2 + 2 * n_blk]
    bias_ref = refs[3 + 2 * n_blk]
    i = pl.program_id(0)
    cpb = tq // CHUNK

    @pl.when(i == 0)
    def _():
        rq = lax.broadcasted_iota(jnp.int32, (tq, tq), 0) // CHUNK + (n_blk - 1) * cpb
        cc = lax.broadcasted_iota(jnp.int32, (tq, tq), 1) // CHUNK
        for h in range(n_heads):
            ub = jnp.broadcast_to(u_ref[h], (tq, u_ref.shape[-1]))
            t = pltpu.roll(ub, 0, axis=1, stride=1, stride_axis=0)
            for b in range(n_blk):
                ck = cc + b * cpb
                band = (ck <= rq) & (ck >= rq - LEFT_CHUNKS)
                bias_ref[h, b] = jnp.where(band, t[:, b * tq:(b + 1) * tq], NEG_INF)
            bias_ref[h, n_blk] = jnp.full((tq, tq), NEG_INF, jnp.float32)

    bias_idx = [jnp.where(i - (n_blk - 1) + b >= 0, b, n_blk) for b in range(n_blk - 1)] + [n_blk - 1]
    nt = (((1,), (1,)), ((), ()))

    def head_cols(h):
        return slice(h * HEAD_DIM, (h + 1) * HEAD_DIM)

    def scores(h):
        q = q_ref[:, head_cols(h)]
        return [lax.dot_general(q, k_refs[b][:, head_cols(h)], nt, preferred_element_type=jnp.float32)
                + bias_ref[h, bias_idx[b]] for b in range(n_blk)]

    half = tq // 2
    cph = cpb // 2

    def quad_visible(b, rh, ch):
        return any(0 <= (n_blk - 1) * cpb + rh * cph + a - (b * cpb + ch * cph + c) <= LEFT_CHUNKS
                   for a in range(cph) for c in range(cph))

    def softmax(s):
        out = [[None, None] for _ in range(n_blk)]
        for rh in range(2):
            rows = slice(rh * half, (rh + 1) * half)
            pieces = {(b, ch): s[b][rows, ch * half:(ch + 1) * half]
                      for b in range(n_blk) for ch in range(2) if quad_visible(b, rh, ch)}
            m = jnp.max(functools.reduce(jnp.maximum, pieces.values()), axis=-1, keepdims=True)
            for b in range(n_blk):
                out[b][rh] = jnp.concatenate(
                    [jnp.exp(pieces[(b, ch)] - m).astype(MXU_DTYPE) if (b, ch) in pieces
                     else jnp.zeros((half, half), MXU_DTYPE) for ch in range(2)], axis=1)
        return [jnp.concatenate(ob, axis=0) for ob in out]

    ones = jnp.ones((tq, HEAD_DIM), MXU_DTYPE)

    def weighted_values(h, p):
        ov = functools.reduce(
            lambda a, c: a + c,
            [jnp.dot(p[b], jnp.concatenate([v_refs[b][:, head_cols(h)], ones], axis=1),
                     preferred_element_type=jnp.float32) for b in range(n_blk)])
        o, l = ov[:, :HEAD_DIM], ov[:, HEAD_DIM:HEAD_DIM + 1]
        o_ref[:, head_cols(h)] = (o * (1.0 / l)).astype(o_ref.dtype)

    s_next = scores(0)
    for h in range(n_heads):
        s = s_next
        if h + 1 < n_heads:
            s_next = scores(h + 1)
        weighted_values(h, softmax(s))


def _bias_rows(rel_table, tq, width):
    n_rel = rel_table.shape[-1]
    far = LEFT - REL_MAX
    t = rel_table.astype(jnp.float32)
    oldest = jnp.broadcast_to(t[:, n_rel - 1:], (t.shape[0], far))
    newest = jnp.broadcast_to(t[:, :1], (t.shape[0], width - tq - far - n_rel))
    wrapped = jnp.broadcast_to(t[:, n_rel - 1:], (t.shape[0], tq))
    u = jnp.concatenate([oldest, t[:, ::-1], newest, wrapped], axis=1)
    return u.reshape(t.shape[0], 1, width)


def _attn(q, k, v, rel_table):
    s, d = q.shape
    n_heads = d // HEAD_DIM
    tq = ATTN_TQ
    n_blk = LEFT // tq + 1
    width = (n_blk + 1) * tq
    u = _bias_rows(rel_table, tq, width)
    kern = functools.partial(_attn_kernel, n_blk=n_blk, tq=tq, n_heads=n_heads)
    q_spec = pl.BlockSpec((tq, d), lambda i: (i, 0))

    def kv_spec(b):
        return pl.BlockSpec((tq, d), lambda i: (jnp.maximum(i - (n_blk - 1) + b, 0), 0))

    kv_specs = [kv_spec(b) for b in range(n_blk)]
    bias_shape = (n_heads, n_blk + 1, tq, tq)
    score_bytes = n_heads * n_blk * _nbytes((tq, tq), jnp.float32)
    vmem = ((4 + 4 * n_blk) * _nbytes((tq, d), MXU_DTYPE) + _nbytes(bias_shape, jnp.float32)
            + _nbytes(u.shape, jnp.float32) + score_bytes)
    return pl.pallas_call(
        kern,
        grid=(s // tq,),
        in_specs=[q_spec] + kv_specs + kv_specs + [_resident(u.shape, lambda i: (0, 0, 0))],
        out_specs=q_spec,
        out_shape=jax.ShapeDtypeStruct((s, d), MXU_DTYPE),
        scratch_shapes=[pltpu.VMEM(bias_shape, jnp.float32)],
        compiler_params=_params(("arbitrary",), vmem),
        name="attn",
    )(q, *([k] * n_blk), *([v] * n_blk), u)


def _oproj_kernel(a_ref, w_ref, x_ref, o_ref, wc_ref, *, tn):
    _cast_weights_once([w_ref], [wc_ref], tn)
    a = a_ref[...]
    for c in range(0, wc_ref.shape[1], tn):
        o_ref[:, c:c + tn] = x_ref[:, c:c + tn] + jnp.dot(
            a, wc_ref[:, c:c + tn], preferred_element_type=jnp.float32)


def _oproj(a, w, w_layer, x):
    s, d = x.shape
    tm = PROJ_TM
    kd = a.shape[1]
    vmem = (2 * _nbytes((tm, kd), MXU_DTYPE) + _nbytes((kd, d), w.dtype) + _nbytes((kd, d), MXU_DTYPE)
            + 4 * _nbytes((tm, d), jnp.float32))
    return pl.pallas_call(
        functools.partial(_oproj_kernel, tn=PROJ_TN),
        grid=(s // tm,),
        in_specs=[
            pl.BlockSpec((tm, kd), lambda i: (i, 0)),
            _resident((None, kd, d), lambda i: (w_layer, 0, 0)),
            pl.BlockSpec((tm, d), lambda i: (i, 0)),
        ],
        out_specs=pl.BlockSpec((tm, d), lambda i: (i, 0)),
        out_shape=jax.ShapeDtypeStruct((s, d), jnp.float32),
        scratch_shapes=[pltpu.VMEM((kd, d), MXU_DTYPE)],
        compiler_params=_params(("arbitrary",), vmem),
        name="oproj",
    )(a, w, x)


def kernel(x, ffn1_norm, ffn1_w_gate, ffn1_w_up, ffn1_w_down, mix_norm, ffn2_norm, ffn2_w_gate, ffn2_w_up, ffn2_w_down, pool_w, pool_scale, kv_norm, w_k, w_v, k_gain, w_q, q_gain, rel_bias, w_o):
    b, s, d = x.shape
    depth = ffn1_norm.shape[0]
    n_a = pool_w.shape[0]
    cast = lambda w: w.astype(MXU_DTYPE)
    f1 = (ffn1_norm, ffn1_w_gate, ffn1_w_up, ffn1_w_down)
    f2 = (ffn2_norm, ffn2_w_gate, ffn2_w_up, ffn2_w_down)
    pool_wc = cast(pool_w)
    wk, wv = cast(w_k)[None], cast(w_v)[None]
    kv_g = kv_norm.reshape(1, d)

    outs = []
    for bi in range(b):
        xs = x.reshape(b * s, d) if b == 1 else x[bi]
        k = v = None
        for l in range(depth):
            xs = _ffn(xs, *f1, l, keep_x=(l == 0))
            if l < n_a:
                xs = _pool(xs, mix_norm, pool_wc, pool_scale, l)
            else:
                a = l - n_a
                (q,) = _proj(xs, mix_norm, l, q_gain[a].reshape(1, HEAD_DIM), (w_q,), a, (True,),
                             out_scale=HEAD_DIM ** -0.5)
                o = _attn(q, k, v, rel_bias[a])
                xs = _oproj(o, w_o, a, xs)
            xs = _ffn(xs, *f2, l)
            if l == n_a - 1:
                k, v = _proj(xs, kv_g, 0, k_gain.reshape(1, HEAD_DIM), (wk, wv), 0, (True, False))
        outs.append(xs)
    return outs[0].reshape(b, s, d) if b == 1 else jnp.stack(outs, axis=0)
```
